```python
import math
import jax, jax.numpy as jnp
from jax import lax
import numpy as np

D_MODEL = 2048
BATCH = 4
SEQ = 4096
DEPTH = 1
DEC_BATCH = 16
DEC_SEQ = 64
PAST_LEN = 2048

CHUNK = 64
Q_BLOCK = 128
MIX_WIDTH = D_MODEL
ATTN_WIDTH = MIX_WIDTH // 2
CONV_CH = MIX_WIDTH - ATTN_WIDTH
N_HEADS = 8
HEAD_DIM = ATTN_WIDTH // (2 * N_HEADS)
KEY_DIM = 2 * HEAD_DIM
VALUE_DIM = 2 * HEAD_DIM
CONV_WIDTH = 3
FFN_DIM = ((8 * D_MODEL // 3 + 255) // 256) * 256
PROJ_DIM = 3 * ATTN_WIDTH + 3 * CONV_CH
ROPE_THETA = 10000.0
EPS = 1e-6

kernel_name = "hybrid_diffattn_shortconv_macaron_step"


def rmsnorm(x, g):
    xf = x.astype(jnp.float32)
    y = xf * lax.rsqrt(jnp.mean(xf * xf, axis=-1, keepdims=True) + EPS)
    return (y * g.astype(jnp.float32)).astype(x.dtype)


def swiglu(x, w_gate, w_up, w_down):
    return (jax.nn.silu(x @ w_gate) * (x @ w_up)) @ w_down


def rope(x, pos):
    half = HEAD_DIM // 2
    inv = ROPE_THETA ** (-jnp.arange(0, HEAD_DIM, 2, dtype=jnp.float32) / HEAD_DIM)
    ang = pos.astype(jnp.float32)[:, None] * inv[None, :]
    cos = jnp.cos(ang)[:, None, None, :]
    sin = jnp.sin(ang)[:, None, None, :]
    xf = x.astype(jnp.float32)
    x1, x2 = xf[..., :half], xf[..., half:]
    return jnp.concatenate([x1 * cos - x2 * sin, x2 * cos + x1 * sin], axis=-1).astype(x.dtype)


def diff_attend(q, k, v, q_pos, k_pos, lam):
    s = jnp.einsum('bqhcd,bkhcd->bhcqk', q, k).astype(jnp.float32) * (HEAD_DIM ** -0.5)
    visible = k_pos[None, :] < (q_pos[:, None] // CHUNK + 1) * CHUNK
    s = jnp.where(visible, s, -jnp.inf)
    p = jax.nn.softmax(s, axis=-1)
    a = p[:, :, 0] - lam * p[:, :, 1]
    return jnp.einsum('bhqk,bkhe->bqhe', a.astype(v.dtype), v)


def prompt_attention(q, k, v, lam):
    b, s = q.shape[0], q.shape[1]
    nb = s // Q_BLOCK
    pos = jnp.arange(s)
    qb = q.reshape(b, nb, Q_BLOCK, N_HEADS, 2, HEAD_DIM).transpose(1, 0, 2, 3, 4, 5)
    pb = pos.reshape(nb, Q_BLOCK)
    out = lax.map(lambda a: diff_attend(a[0], k, v, a[1], pos, lam), (qb, pb))
    return out.transpose(1, 0, 2, 3, 4).reshape(b, s, N_HEADS, VALUE_DIM)


def short_conv(u_hist, conv_w, length):
    return sum(conv_w[j] * u_hist[:, j:j + length] for j in range(CONV_WIDTH))


def trunk_layer(x, pos, k_past, v_past, u_past, ffn1_norm, ffn1_w_gate, ffn1_w_up,
                ffn1_w_down, mix_norm, w_in, lambda_q1, lambda_k1, lambda_q2, lambda_k2,
                subln_gain, conv_w, w_out, ffn2_norm, ffn2_w_gate, ffn2_w_up,
                ffn2_w_down, lambda_init):
    b, length = x.shape[0], x.shape[1]
    x = x + 0.5 * swiglu(rmsnorm(x, ffn1_norm), ffn1_w_gate, ffn1_w_up, ffn1_w_down)
    h = rmsnorm(x, mix_norm)
    proj = h @ w_in
    a_w, c_w = ATTN_WIDTH, CONV_CH
    q, k, v, cb, cc, ch = jnp.split(
        proj, [a_w, 2 * a_w, 3 * a_w, 3 * a_w + c_w, 3 * a_w + 2 * c_w], axis=-1)
    q = rope(q.reshape(b, length, N_HEADS, 2, HEAD_DIM), pos)
    k = rope(k.reshape(b, length, N_HEADS, 2, HEAD_DIM), pos)
    v = v.reshape(b, length, N_HEADS, VALUE_DIM)
    lam = (jnp.exp(jnp.sum(lambda_q1.astype(jnp.float32) * lambda_k1.astype(jnp.float32)))
           - jnp.exp(jnp.sum(lambda_q2.astype(jnp.float32) * lambda_k2.astype(jnp.float32)))
           + lambda_init)
    if k_past is None:
        o = prompt_attention(q, k, v, lam)
    else:
        past = k_past.shape[1]
        k_all = jnp.concatenate([k_past.reshape(b, past, N_HEADS, 2, HEAD_DIM), k], axis=1)
        v_all = jnp.concatenate([v_past, v], axis=1)
        o = diff_attend(q, k_all, v_all, pos, jnp.arange(past + length), lam)
    attn_out = (rmsnorm(o, subln_gain) * (1.0 - lambda_init)).reshape(b, length, ATTN_WIDTH)
    u = cc * ch
    u_hist = jnp.concatenate([u_past, u], axis=1)
    conv_out = cb * short_conv(u_hist, conv_w, length)
    x = x + jnp.concatenate([attn_out, conv_out], axis=-1) @ w_out
    x = x + 0.5 * swiglu(rmsnorm(x, ffn2_norm), ffn2_w_gate, ffn2_w_up, ffn2_w_down)
    new_k = k.reshape(b, length, N_HEADS, KEY_DIM)
    return x, new_k, v, u_hist[:, -(CONV_WIDTH - 1):]


def setup_inputs(seed: int = 0) -> dict:
    key = jax.random.key(seed)
    ks = jax.random.split(key, 24)
    f32 = jnp.float32
    nrm = lambda k, shape, s: jax.random.normal(k, shape, f32) * s
    gain = lambda k, shape: 1.0 + 0.01 * jax.random.normal(k, shape, f32)
    return {
        "x_prompt": nrm(ks[0], (BATCH, SEQ, D_MODEL), 1.0),
        "x_sample": nrm(ks[1], (DEC_BATCH, DEC_SEQ, D_MODEL), 1.0),
        "cache_k": nrm(ks[2], (DEPTH, DEC_BATCH, PAST_LEN, N_HEADS, KEY_DIM), 1.0),
        "cache_v": nrm(ks[3], (DEPTH, DEC_BATCH, PAST_LEN, N_HEADS, VALUE_DIM), 1.0),
        "state_conv": nrm(ks[4], (DEPTH, DEC_BATCH, CONV_WIDTH - 1, CONV_CH), 1.0),
        "ffn1_norm": gain(ks[5], (DEPTH, D_MODEL)),
        "ffn1_w_gate": nrm(ks[6], (DEPTH, D_MODEL, FFN_DIM), D_MODEL ** -0.5),
        "ffn1_w_up": nrm(ks[7], (DEPTH, D_MODEL, FFN_DIM), D_MODEL ** -0.5),
        "ffn1_w_down": nrm(ks[8], (DEPTH, FFN_DIM, D_MODEL), FFN_DIM ** -0.5),
        "mix_norm": gain(ks[9], (DEPTH, D_MODEL)),
        "w_in": nrm(ks[10], (DEPTH, D_MODEL, PROJ_DIM), D_MODEL ** -0.5),
        "lambda_q1": nrm(ks[11], (DEPTH, HEAD_DIM), 0.1),
        "lambda_k1": nrm(ks[12], (DEPTH, HEAD_DIM), 0.1),
        "lambda_q2": nrm(ks[13], (DEPTH, HEAD_DIM), 0.1),
        "lambda_k2": nrm(ks[14], (DEPTH, HEAD_DIM), 0.1),
        "subln_gain": gain(ks[15], (DEPTH, VALUE_DIM)),
        "conv_w": nrm(ks[16], (DEPTH, CONV_WIDTH, CONV_CH), CONV_WIDTH ** -0.5),
        "w_out": nrm(ks[17], (DEPTH, MIX_WIDTH, D_MODEL), MIX_WIDTH ** -0.5),
        "ffn2_norm": gain(ks[18], (DEPTH, D_MODEL)),
        "ffn2_w_gate": nrm(ks[19], (DEPTH, D_MODEL, FFN_DIM), D_MODEL ** -0.5),
        "ffn2_w_up": nrm(ks[20], (DEPTH, D_MODEL, FFN_DIM), D_MODEL ** -0.5),
        "ffn2_w_down": nrm(ks[21], (DEPTH, FFN_DIM, D_MODEL), FFN_DIM ** -0.5),
        "final_norm": gain(ks[22], (D_MODEL,)),
    }


def reference(x_prompt, x_sample, cache_k, cache_v, state_conv, ffn1_norm, ffn1_w_gate,
              ffn1_w_up, ffn1_w_down, mix_norm, w_in, lambda_q1, lambda_k1, lambda_q2,
              lambda_k2, subln_gain, conv_w, w_out, ffn2_norm, ffn2_w_gate, ffn2_w_up,
              ffn2_w_down, final_norm):
    xp, xs = x_prompt, x_sample
    pos_p = jnp.arange(xp.shape[1])
    pos_s = cache_k.shape[2] + jnp.arange(xs.shape[1])
    kp_l, vp_l, cp_l, ks_l, vs_l, cs_l = [], [], [], [], [], []
    for l in range(DEPTH):
        lambda_init = 0.8 - 0.6 * math.exp(-0.3 * l)
        w = (ffn1_norm[l], ffn1_w_gate[l], ffn1_w_up[l], ffn1_w_down[l], mix_norm[l],
             w_in[l], lambda_q1[l], lambda_k1[l], lambda_q2[l], lambda_k2[l],
             subln_gain[l], conv_w[l], w_out[l], ffn2_norm[l], ffn2_w_gate[l],
             ffn2_w_up[l], ffn2_w_down[l], lambda_init)
        u0 = jnp.zeros((xp.shape[0], CONV_WIDTH - 1, CONV_CH), xp.dtype)
        xp, kp, vp, cp = trunk_layer(xp, pos_p, None, None, u0, *w)
        xs, ksn, vsn, csn = trunk_layer(xs, pos_s, cache_k[l], cache_v[l], state_conv[l], *w)
        kp_l.append(kp); vp_l.append(vp); cp_l.append(cp)
        ks_l.append(ksn); vs_l.append(vsn); cs_l.append(csn)
    y_prompt = rmsnorm(xp, final_norm)
    y_sample = rmsnorm(xs, final_norm)
    k_prompt = jnp.stack(kp_l)
    v_prompt = jnp.stack(vp_l)
    conv_prompt = jnp.stack(cp_l)
    k_sample = jnp.stack(ks_l)
    v_sample = jnp.stack(vs_l)
    conv_sample = jnp.stack(cs_l)
    return (y_prompt, y_sample, k_prompt, v_prompt, conv_prompt, k_sample, v_sample, conv_sample)
```

```python
import functools
import math

import jax
import jax.numpy as jnp
from jax import lax
from jax.experimental import pallas as pl
from jax.experimental.pallas import tpu as pltpu

F32 = jnp.float32
BF16 = jnp.bfloat16

CHUNK = 64
ROPE_THETA = 10000.0
EPS = 1e-6
MASK_VALUE = -1e30

V7X_LANES = 128
V7X_VMEM_BYTES = 64 * 1024 * 1024
VMEM_BUDGET_BYTES = V7X_VMEM_BYTES * 7 // 8

FFN_TOKEN_TILE = 512
FFN_COL_TILE = 512
INPROJ_TOKEN_TILE = 256
OUTPROJ_TOKEN_TILE = 512
ATTN_BLOCK = 512


def _nbytes(shape, dtype):
    return math.prod(shape) * jnp.dtype(dtype).itemsize


def _vmem_limit(pipelined, resident):
    need = 2 * sum(_nbytes(s, d) for s, d in pipelined) + sum(_nbytes(s, d) for s, d in resident)
    return int(min(VMEM_BUDGET_BYTES, max(need, 16 * 1024 * 1024)))


def _rms(x, gain):
    return (x * lax.rsqrt(jnp.mean(x * x, axis=-1, keepdims=True) + EPS)) * gain


def _ffn_kernel(x_ref, nw_ref, wg_ref, wu_ref, wd_ref, fn_ref, o_ref, h_ref, *, apply_final_norm):
    f = pl.program_id(1)

    @pl.when(f == 0)
    def _():
        h_ref[...] = _rms(x_ref[...], nw_ref[...]).astype(BF16)
        o_ref[...] = jnp.zeros_like(o_ref)

    h = h_ref[...]
    gate = jnp.dot(h, wg_ref[...], preferred_element_type=F32)
    up = jnp.dot(h, wu_ref[...], preferred_element_type=F32)
    act = ((gate * jax.nn.sigmoid(gate)) * up).astype(BF16)
    o_ref[...] += jnp.dot(act, wd_ref[...], preferred_element_type=F32)

    @pl.when(f == pl.num_programs(1) - 1)
    def _():
        y = x_ref[...] + 0.5 * o_ref[...]
        if apply_final_norm:
            y = _rms(y, fn_ref[...])
        o_ref[...] = y


def _ffn(x, norm_w, w_gate, w_up, w_down, final_w, apply_final_norm):
    t, d = x.shape
    ffn = w_gate.shape[1]
    tm, tf = min(FFN_TOKEN_TILE, t), FFN_COL_TILE
    assert t % tm == 0 and ffn % tf == 0
    vmem = _vmem_limit(
        pipelined=[((tm, d), F32), ((tm, d), F32), ((d, tf), BF16), ((d, tf), BF16), ((tf, d), BF16)],
        resident=[((tm, d), BF16), ((tm, tf), F32), ((tm, tf), F32), ((tm, tf), F32), ((tm, d), F32)])
    return pl.pallas_call(
        functools.partial(_ffn_kernel, apply_final_norm=apply_final_norm),
        name="ffn",
        grid=(t // tm, ffn // tf),
        in_specs=[
            pl.BlockSpec((tm, d), lambda i, f: (i, 0)),
            pl.BlockSpec((1, d), lambda i, f: (0, 0)),
            pl.BlockSpec((d, tf), lambda i, f: (0, f)),
            pl.BlockSpec((d, tf), lambda i, f: (0, f)),
            pl.BlockSpec((tf, d), lambda i, f: (f, 0)),
            pl.BlockSpec((1, d), lambda i, f: (0, 0)),
        ],
        out_specs=pl.BlockSpec((tm, d), lambda i, f: (i, 0)),
        out_shape=jax.ShapeDtypeStruct((t, d), F32),
        scratch_shapes=[pltpu.VMEM((tm, d), BF16)],
        compiler_params=pltpu.CompilerParams(
            dimension_semantics=("parallel", "arbitrary"), vmem_limit_bytes=vmem),
    )(x, norm_w, w_gate, w_up, w_down, final_w)


def _inproj_kernel(x_ref, nw_ref, w_ref, cos_ref, sin_lo_ref, sin_hi_ref, cw_ref, st_ref,
                   q_ref, k_ref, v_ref, co_ref, sto_ref, carry_ref,
                   *, width, n_heads, n_seq, tiles_per_seq, q_scale):
    i = pl.program_id(0)
    tm = x_ref.shape[0]
    seq_rows = tm // n_seq
    h = _rms(x_ref[...], nw_ref[...]).astype(BF16)

    def proj(col):
        return jnp.dot(h, w_ref[:, col * width:(col + 1) * width], preferred_element_type=F32)

    cos, sin_lo, sin_hi = cos_ref[...], sin_lo_ref[...], sin_hi_ref[...]

    def rope_store(x, out_ref, scale):
        for hd in range(n_heads):
            xh = x[:, hd * V7X_LANES:(hd + 1) * V7X_LANES]
            r = (xh * cos + pltpu.roll(xh, V7X_LANES - 32, axis=1) * sin_lo
                 + pltpu.roll(xh, 32, axis=1) * sin_hi)
            if scale is not None:
                r = r * scale
            out_ref[:, hd * V7X_LANES:(hd + 1) * V7X_LANES] = r.astype(out_ref.dtype)

    rope_store(proj(0), q_ref, q_scale)
    rope_store(proj(1), k_ref, None)
    v_ref[...] = proj(2)

    cb = proj(3)
    u = proj(4) * proj(5)
    if tiles_per_seq > 1:
        @pl.when(i % tiles_per_seq == 0)
        def _():
            carry_ref[0:2, :] = st_ref[0]
    for s in range(n_seq):
        rows = slice(s * seq_rows, (s + 1) * seq_rows)
        us = u[rows]
        if tiles_per_seq > 1:
            hist0, hist1 = carry_ref[0:1, :], carry_ref[1:2, :]
        else:
            hist0, hist1 = st_ref[s, 0:1, :], st_ref[s, 1:2, :]
        row = lax.broadcasted_iota(jnp.int32, us.shape, 0)
        back1 = pltpu.roll(us, 1, axis=0)
        back2 = pltpu.roll(us, 2, axis=0)
        u1 = jnp.where(row == 0, hist1, back1)
        u2 = jnp.where(row == 0, hist0, jnp.where(row == 1, hist1, back2))
        conv = cw_ref[0:1, :] * u2 + cw_ref[1:2, :] * u1 + cw_ref[2:3, :] * us
        co_ref[rows, :] = (cb[rows] * conv).astype(co_ref.dtype)
        last = back2[0:2, :]
        sto_ref[s] = last
        if tiles_per_seq > 1:
            carry_ref[0:2, :] = last


def _inproj(x, norm_w, w_in, rope_tabs, conv_w, conv_state, seq_len, n_heads, q_scale):
    t, d = x.shape
    width = w_in.shape[1] // 6
    n_batch = conv_state.shape[0]
    hist = conv_state.shape[1]
    assert hist == 2 and seq_len >= hist and t == n_batch * seq_len
    tm = INPROJ_TOKEN_TILE
    if seq_len >= tm:
        assert seq_len % tm == 0
        n_seq, tiles_per_seq = 1, seq_len // tm
    else:
        assert tm % seq_len == 0 and seq_len % 8 == 0
        n_seq, tiles_per_seq = tm // seq_len, 1
    cos, sin_lo, sin_hi = rope_tabs
    pos_tiles = cos.shape[0] // tm
    tab_spec = pl.BlockSpec((tm, V7X_LANES), lambda i: (i % pos_tiles, 0))
    tok = lambda i: (i, 0)
    const = lambda i: (0, 0)
    st_spec = pl.BlockSpec((n_seq, hist, width), lambda i: (i // tiles_per_seq, 0, 0))
    vmem = _vmem_limit(
        pipelined=[((tm, d), F32), ((tm, width), BF16), ((tm, width), F32), ((tm, width), F32),
                   ((tm, width), BF16), ((tm, V7X_LANES), F32), ((tm, V7X_LANES), F32),
                   ((tm, V7X_LANES), F32)],
        resident=[(w_in.shape, BF16), ((tm, d), BF16)] + [((tm, width), F32)] * 8)
    return pl.pallas_call(
        functools.partial(_inproj_kernel, width=width, n_heads=n_heads, n_seq=n_seq,
                          tiles_per_seq=tiles_per_seq, q_scale=q_scale),
        name="inproj",
        grid=(t // tm,),
        in_specs=[
            pl.BlockSpec((tm, d), tok),
            pl.BlockSpec((1, d), const),
            pl.BlockSpec(w_in.shape, const, pipeline_mode=pl.Buffered(1)),
            tab_spec, tab_spec, tab_spec,
            pl.BlockSpec(conv_w.shape, const),
            st_spec,
        ],
        out_specs=[
            pl.BlockSpec((tm, width), tok),
            pl.BlockSpec((tm, width), tok),
            pl.BlockSpec((tm, width), tok),
            pl.BlockSpec((tm, width), tok),
            st_spec,
        ],
        out_shape=[
            jax.ShapeDtypeStruct((t, width), BF16),
            jax.ShapeDtypeStruct((t, width), F32),
            jax.ShapeDtypeStruct((t, width), F32),
            jax.ShapeDtypeStruct((t, width), BF16),
            jax.ShapeDtypeStruct(conv_state.shape, F32),
        ],
        scratch_shapes=[pltpu.VMEM((8, width), F32)],
        compiler_params=pltpu.CompilerParams(
            dimension_semantics=("arbitrary",), vmem_limit_bytes=vmem),
    )(x, norm_w, w_in, cos, sin_lo, sin_hi, conv_w, conv_state)


def _stack_components(q):
    lane = lax.broadcasted_iota(jnp.int32, q.shape, 1)
    half = q.shape[1] // 2
    zero = jnp.zeros_like(q)
    return jnp.concatenate([jnp.where(lane < half, q, zero), jnp.where(lane >= half, q, zero)], axis=0)


def _visible(shape, q_start, k_start):
    assert CHUNK & (CHUNK - 1) == 0
    rows = shape[0] // 2
    r = lax.broadcasted_iota(jnp.int32, shape, 0)
    q_pos = q_start + jnp.where(r >= rows, r - rows, r)
    k_pos = k_start + lax.broadcasted_iota(jnp.int32, shape, 1)
    return k_pos < (q_pos | (CHUNK - 1)) + 1


def _scores(q2, k):
    return lax.dot_general(q2, k, (((1,), (1,)), ((), ())), preferred_element_type=F32)


def _lambda(lq1_ref, lk1_ref, lq2_ref, lk2_ref, lambda_init):
    s1 = jnp.sum(lq1_ref[...] * lk1_ref[...], axis=1, keepdims=True)
    s2 = jnp.sum(lq2_ref[...] * lk2_ref[...], axis=1, keepdims=True)
    return jnp.exp(s1) - jnp.exp(s2) + lambda_init


def _diff_finish(acc, l, lam, gain, lambda_init):
    rows = acc.shape[0] // 2
    o = acc[:rows] / l[:rows] - lam * (acc[rows:] / l[rows:])
    return _rms(o, gain) * (1.0 - lambda_init)


def _attn_prompt_kernel(q_ref, k_ref, v_ref, lq1_ref, lk1_ref, lq2_ref, lk2_ref, g_ref, o_ref,
                        m_ref, l_ref, acc_ref, *, lambda_init):
    i = pl.program_id(2)
    blk = q_ref.shape[0]
    q2 = _stack_components(q_ref[...])
    m_ref[...] = jnp.full_like(m_ref, MASK_VALUE)
    l_ref[...] = jnp.zeros_like(l_ref)
    acc_ref[...] = jnp.zeros_like(acc_ref)

    def step(kb, masked):
        start = pl.multiple_of(kb * blk, blk)
        k = k_ref[pl.ds(start, blk), :].astype(BF16)
        v = v_ref[pl.ds(start, blk), :].astype(BF16)
        s = _scores(q2, k)
        if masked:
            s = jnp.where(_visible(s.shape, 0, 0), s, MASK_VALUE)
        m_prev = m_ref[...]
        m_new = jnp.maximum(m_prev, jnp.max(s, axis=1, keepdims=True))
        alpha = jnp.exp(m_prev - m_new)
        p = jnp.exp(s - m_new)
        l_ref[...] = alpha * l_ref[...] + jnp.sum(p, axis=1, keepdims=True)
        acc_ref[...] = alpha * acc_ref[...] + jnp.dot(p.astype(BF16), v, preferred_element_type=F32)
        m_ref[...] = m_new

    def body(kb, carry):
        step(kb, False)
        return carry

    lax.fori_loop(0, i, body, 0)
    step(i, True)

    lam = _lambda(lq1_ref, lk1_ref, lq2_ref, lk2_ref, lambda_init)
    o_ref[...] = _diff_finish(acc_ref[...], l_ref[...], lam, g_ref[...], lambda_init).astype(o_ref.dtype)


def _attn_prompt(q, k, v, lams, gain, n_batch, seq_len, n_heads, lambda_init):
    t, width = q.shape
    e = width // n_heads
    blk = ATTN_BLOCK
    assert seq_len % blk == 0 and blk % CHUNK == 0 and e == V7X_LANES
    nq = seq_len // blk
    lam_spec = pl.BlockSpec(lams[0].shape, lambda b, h, i: (0, 0))
    vmem = _vmem_limit(
        pipelined=[((blk, e), BF16), ((seq_len, e), F32), ((seq_len, e), F32), ((blk, e), BF16)],
        resident=[((2 * blk, blk), F32)] * 4 + [((2 * blk, e), F32)] * 4)
    return pl.pallas_call(
        functools.partial(_attn_prompt_kernel, lambda_init=lambda_init),
        name="attn_prompt",
        grid=(n_batch, n_heads, nq),
        in_specs=[
            pl.BlockSpec((blk, e), lambda b, h, i: (b * nq + i, h)),
            pl.BlockSpec((seq_len, e), lambda b, h, i: (b, h)),
            pl.BlockSpec((seq_len, e), lambda b, h, i: (b, h)),
            lam_spec, lam_spec, lam_spec, lam_spec,
            pl.BlockSpec(gain.shape, lambda b, h, i: (0, 0)),
        ],
        out_specs=pl.BlockSpec((blk, e), lambda b, h, i: (b * nq + i, h)),
        out_shape=jax.ShapeDtypeStruct((t, width), BF16),
        scratch_shapes=[pltpu.VMEM((2 * blk, 1), F32), pltpu.VMEM((2 * blk, 1), F32),
                        pltpu.VMEM((2 * blk, e), F32)],
        compiler_params=pltpu.CompilerParams(
            dimension_semantics=("parallel", "parallel", "arbitrary"), vmem_limit_bytes=vmem),
    )(q, k, v, *lams, gain)


def _attn_sample_kernel(q_ref, kn_ref, vn_ref, kc_ref, vc_ref, lq1_ref, lk1_ref, lq2_ref, lk2_ref,
                        g_ref, o_ref, *, n_heads, past_len, lambda_init):
    rows = q_ref.shape[0]
    e = q_ref.shape[1] // n_heads
    lam = _lambda(lq1_ref, lk1_ref, lq2_ref, lk2_ref, lambda_init)
    gain = g_ref[...]
    visible = _visible((2 * rows, rows), past_len, past_len)
    for hd in range(n_heads):
        cols = slice(hd * e, (hd + 1) * e)
        q2 = _stack_components(q_ref[:, cols])
        s_c = _scores(q2, kc_ref[:, cols].astype(BF16))
        s_n = jnp.where(visible, _scores(q2, kn_ref[:, cols].astype(BF16)), MASK_VALUE)
        m = jnp.maximum(jnp.max(s_c, axis=1, keepdims=True), jnp.max(s_n, axis=1, keepdims=True))
        p_c = jnp.exp(s_c - m)
        p_n = jnp.exp(s_n - m)
        l = jnp.sum(p_c, axis=1, keepdims=True) + jnp.sum(p_n, axis=1, keepdims=True)
        acc = (jnp.dot(p_c.astype(BF16), vc_ref[:, cols].astype(BF16), preferred_element_type=F32)
               + jnp.dot(p_n.astype(BF16), vn_ref[:, cols].astype(BF16), preferred_element_type=F32))
        o_ref[:, cols] = _diff_finish(acc, l, lam, gain, lambda_init).astype(o_ref.dtype)


def _attn_sample(q, k_new, v_new, k_cache, v_cache, lams, gain, n_batch, seq_len, n_heads,
                 lambda_init):
    t, width = q.shape
    past_len = k_cache.shape[0] // n_batch
    assert seq_len % 8 == 0 and past_len % 8 == 0
    tok = lambda b: (b, 0)
    lam_spec = pl.BlockSpec(lams[0].shape, lambda b: (0, 0))
    vmem = _vmem_limit(
        pipelined=[((past_len, width), F32)] * 2 + [((seq_len, width), F32)] * 4,
        resident=[((2 * seq_len, past_len), F32)] * 4 + [((past_len, width // n_heads), BF16)] * 4)
    return pl.pallas_call(
        functools.partial(_attn_sample_kernel, n_heads=n_heads, past_len=past_len,
                          lambda_init=lambda_init),
        name="attn_sample",
        grid=(n_batch,),
        in_specs=[
            pl.BlockSpec((seq_len, width), tok),
            pl.BlockSpec((seq_len, width), tok),
            pl.BlockSpec((seq_len, width), tok),
            pl.BlockSpec((past_len, width), tok),
            pl.BlockSpec((past_len, width), tok),
            lam_spec, lam_spec, lam_spec, lam_spec,
            pl.BlockSpec(gain.shape, lambda b: (0, 0)),
        ],
        out_specs=pl.BlockSpec((seq_len, width), tok),
        out_shape=jax.ShapeDtypeStruct((t, width), BF16),
        compiler_params=pltpu.CompilerParams(
            dimension_semantics=("parallel",), vmem_limit_bytes=vmem),
    )(q, k_new, v_new, k_cache, v_cache, *lams, gain)


def _outproj_kernel(x_ref, a_ref, c_ref, w_ref, o_ref):
    mixed = jnp.concatenate([a_ref[...], c_ref[...]], axis=1)
    o_ref[...] = x_ref[...] + jnp.dot(mixed, w_ref[...], preferred_element_type=F32)


def _outproj(x, attn_out, conv_out, w_out):
    t, d = x.shape
    wa, wc = attn_out.shape[1], conv_out.shape[1]
    tm = min(OUTPROJ_TOKEN_TILE, t)
    assert t % tm == 0 and wa + wc == w_out.shape[0]
    tok = lambda i: (i, 0)
    vmem = _vmem_limit(
        pipelined=[((tm, d), F32), ((tm, wa), BF16), ((tm, wc), BF16), (w_out.shape, BF16),
                   ((tm, d), F32)],
        resident=[((tm, wa + wc), BF16), ((tm, d), F32)])
    return pl.pallas_call(
        _outproj_kernel,
        name="outproj",
        grid=(t // tm,),
        in_specs=[
            pl.BlockSpec((tm, d), tok),
            pl.BlockSpec((tm, wa), tok),
            pl.BlockSpec((tm, wc), tok),
            pl.BlockSpec(w_out.shape, lambda i: (0, 0)),
        ],
        out_specs=pl.BlockSpec((tm, d), tok),
        out_shape=jax.ShapeDtypeStruct((t, d), F32),
        compiler_params=pltpu.CompilerParams(
            dimension_semantics=("parallel",), vmem_limit_bytes=vmem),
    )(x, attn_out, conv_out, w_out)


def _rope_tables(pos, head_dim, rows):
    half = head_dim // 2
    inv = ROPE_THETA ** (-jnp.arange(0, head_dim, 2, dtype=F32) / head_dim)
    ang = pos.astype(F32)[:, None] * inv[None, :]
    cos, sin = jnp.cos(ang), jnp.sin(ang)
    zero = jnp.zeros_like(sin)
    reps = V7X_LANES // head_dim
    cos_t = jnp.tile(jnp.concatenate([cos, cos], axis=1), (rows // pos.shape[0], reps))
    sin_lo = jnp.tile(jnp.concatenate([-sin, zero], axis=1), (rows // pos.shape[0], reps))
    sin_hi = jnp.tile(jnp.concatenate([zero, sin], axis=1), (rows // pos.shape[0], reps))
    return cos_t, sin_lo, sin_hi


def kernel(x_prompt, x_sample, cache_k, cache_v, state_conv, ffn1_norm, ffn1_w_gate, ffn1_w_up,
           ffn1_w_down, mix_norm, w_in, lambda_q1, lambda_k1, lambda_q2, lambda_k2, subln_gain,
           conv_w, w_out, ffn2_norm, ffn2_w_gate, ffn2_w_up, ffn2_w_down, final_norm):
    n_bp, seq_p, d = x_prompt.shape
    n_bs, seq_s, _ = x_sample.shape
    depth, _, past_len, n_heads, key_dim = cache_k.shape
    value_dim = cache_v.shape[-1]
    head_dim = key_dim // 2
    conv_ch = conv_w.shape[-1]
    attn_width = n_heads * key_dim
    assert key_dim == V7X_LANES and value_dim == V7X_LANES and attn_width == conv_ch
    q_scale = head_dim ** -0.5

    tabs_p = _rope_tables(jnp.arange(seq_p), head_dim, seq_p)
    tabs_s = _rope_tables(past_len + jnp.arange(seq_s), head_dim, max(seq_s, INPROJ_TOKEN_TILE))

    xp = x_prompt.reshape(n_bp * seq_p, d)
    xs = x_sample.reshape(n_bs * seq_s, d)
    row = lambda a: a.reshape(1, -1)
    outs = {name: [] for name in ("kp", "vp", "cp", "ks", "vs", "cs")}
    for l in range(depth):
        lambda_init = 0.8 - 0.6 * math.exp(-0.3 * l)
        last = l == depth - 1
        w1g, w1u, w1d = (w[l].astype(BF16) for w in (ffn1_w_gate, ffn1_w_up, ffn1_w_down))
        w2g, w2u, w2d = (w[l].astype(BF16) for w in (ffn2_w_gate, ffn2_w_up, ffn2_w_down))
        w_in_l, w_out_l = w_in[l].astype(BF16), w_out[l].astype(BF16)
        lams = tuple(row(a[l]) for a in (lambda_q1, lambda_k1, lambda_q2, lambda_k2))
        gain = row(subln_gain[l])
        fin = row(final_norm)
        zeros_state = jnp.zeros((n_bp, conv_w.shape[1] - 1, conv_ch), F32)

        def layer(x, tabs, state, seq_len, attend):
            x = _ffn(x, row(ffn1_norm[l]), w1g, w1u, w1d, fin, False)
            q, k, v, conv_out, new_state = _inproj(
                x, row(mix_norm[l]), w_in_l, tabs, conv_w[l], state, seq_len, n_heads, q_scale)
            attn_out = attend(q, k, v)
            x = _outproj(x, attn_out, conv_out, w_out_l)
            x = _ffn(x, row(ffn2_norm[l]), w2g, w2u, w2d, fin, last)
            return x, k, v, new_state

        xp, kp, vp, cp = layer(
            xp, tabs_p, zeros_state, seq_p,
            lambda q, k, v: _attn_prompt(q, k, v, lams, gain, n_bp, seq_p, n_heads, lambda_init))
        kc = cache_k[l].reshape(n_bs * past_len, attn_width)
        vc = cache_v[l].reshape(n_bs * past_len, n_heads * value_dim)
        xs, ks, vs, cs = layer(
            xs, tabs_s, state_conv[l], seq_s,
            lambda q, k, v: _attn_sample(q, k, v, kc, vc, lams, gain, n_bs, seq_s, n_heads,
                                         lambda_init))
        outs["kp"].append(kp.reshape(n_bp, seq_p, n_heads, key_dim))
        outs["vp"].append(vp.reshape(n_bp, seq_p, n_heads, value_dim))
        outs["cp"].append(cp)
        outs["ks"].append(ks.reshape(n_bs, seq_s, n_heads, key_dim))
        outs["vs"].append(vs.reshape(n_bs, seq_s, n_heads, value_dim))
        outs["cs"].append(cs)
    return (xp.reshape(x_prompt.shape), xs.reshape(x_sample.shape),
            jnp.stack(outs["kp"]), jnp.stack(outs["vp"]), jnp.stack(outs["cp"]),
            jnp.stack(outs["ks"]), jnp.stack(outs["vs"]), jnp.stack(outs["cs"]))
```

```python
import functools
import math

import jax
import jax.numpy as jnp
from jax import lax
from jax.experimental import pallas as pl
from jax.experimental.pallas import tpu as pltpu

F32 = jnp.float32
BF16 = jnp.bfloat16

CHUNK = 64
ROPE_THETA = 10000.0
EPS = 1e-6
MASK_VALUE = -1e30

V7X_LANES = 128
V7X_VMEM_BYTES = 64 * 1024 * 1024
VMEM_BUDGET_BYTES = V7X_VMEM_BYTES * 7 // 8

FFN_TOKEN_TILE = 512
FFN_COL_TILE = 512
INPROJ_TOKEN_TILE = 256
OUTPROJ_TOKEN_TILE = 512
ATTN_BLOCK = 512


def _nbytes(shape, dtype):
    return math.prod(shape) * jnp.dtype(dtype).itemsize


def _vmem_limit(pipelined, resident):
    need = 2 * sum(_nbytes(s, d) for s, d in pipelined) + sum(_nbytes(s, d) for s, d in resident)
    return int(min(VMEM_BUDGET_BYTES, max(need, 16 * 1024 * 1024)))


def _rms(x, gain):
    return (x * lax.rsqrt(jnp.mean(x * x, axis=-1, keepdims=True) + EPS)) * gain


def _ffn_kernel(x_ref, nw_ref, wg_ref, wu_ref, wd_ref, fn_ref, o_ref, h_ref, *, apply_final_norm):
    f = pl.program_id(1)

    @pl.when(f == 0)
    def _():
        h_ref[...] = _rms(x_ref[...], nw_ref[...]).astype(BF16)
        o_ref[...] = jnp.zeros_like(o_ref)

    h = h_ref[...]
    gate = jnp.dot(h, wg_ref[...], preferred_element_type=F32)
    up = jnp.dot(h, wu_ref[...], preferred_element_type=F32)
    act = ((gate * jax.nn.sigmoid(gate)) * up).astype(BF16)
    o_ref[...] += jnp.dot(act, wd_ref[...], preferred_element_type=F32)

    @pl.when(f == pl.num_programs(1) - 1)
    def _():
        y = x_ref[...] + 0.5 * o_ref[...]
        if apply_final_norm:
            y = _rms(y, fn_ref[...])
        o_ref[...] = y


def _ffn(x, norm_w, w_gate, w_up, w_down, final_w, apply_final_norm):
    t, d = x.shape
    ffn = w_gate.shape[1]
    tm, tf = min(FFN_TOKEN_TILE, t), FFN_COL_TILE
    assert t % tm == 0 and ffn % tf == 0
    vmem = _vmem_limit(
        pipelined=[((tm, d), F32), ((tm, d), F32), ((d, tf), BF16), ((d, tf), BF16), ((tf, d), BF16)],
        resident=[((tm, d), BF16), ((tm, tf), F32), ((tm, tf), F32), ((tm, tf), F32), ((tm, d), F32)])
    return pl.pallas_call(
        functools.partial(_ffn_kernel, apply_final_norm=apply_final_norm),
        name="ffn",
        grid=(t // tm, ffn // tf),
        in_specs=[
            pl.BlockSpec((tm, d), lambda i, f: (i, 0)),
            pl.BlockSpec((1, d), lambda i, f: (0, 0)),
            pl.BlockSpec((d, tf), lambda i, f: (0, f)),
            pl.BlockSpec((d, tf), lambda i, f: (0, f)),
            pl.BlockSpec((tf, d), lambda i, f: (f, 0)),
            pl.BlockSpec((1, d), lambda i, f: (0, 0)),
        ],
        out_specs=pl.BlockSpec((tm, d), lambda i, f: (i, 0)),
        out_shape=jax.ShapeDtypeStruct((t, d), F32),
        scratch_shapes=[pltpu.VMEM((tm, d), BF16)],
        compiler_params=pltpu.CompilerParams(
            dimension_semantics=("parallel", "arbitrary"), vmem_limit_bytes=vmem),
    )(x, norm_w, w_gate, w_up, w_down, final_w)


def _inproj_kernel(x_ref, nw_ref, w_ref, cos_ref, sin_lo_ref, sin_hi_ref, cw_ref, st_ref,
                   q_ref, k_ref, v_ref, co_ref, sto_ref, carry_ref,
                   *, width, n_heads, n_seq, tiles_per_seq, q_scale):
    i = pl.program_id(0)
    tm = x_ref.shape[0]
    seq_rows = tm // n_seq
    h = _rms(x_ref[...], nw_ref[...]).astype(BF16)

    def proj(col):
        return jnp.dot(h, w_ref[:, col * width:(col + 1) * width], preferred_element_type=F32)

    cos, sin_lo, sin_hi = cos_ref[...], sin_lo_ref[...], sin_hi_ref[...]

    def rope_store(x, out_ref, scale):
        for hd in range(n_heads):
            xh = x[:, hd * V7X_LANES:(hd + 1) * V7X_LANES]
            r = (xh * cos + pltpu.roll(xh, V7X_LANES - 32, axis=1) * sin_lo
                 + pltpu.roll(xh, 32, axis=1) * sin_hi)
            if scale is not None:
                r = r * scale
            out_ref[:, hd * V7X_LANES:(hd + 1) * V7X_LANES] = r.astype(out_ref.dtype)

    rope_store(proj(0), q_ref, q_scale)
    rope_store(proj(1), k_ref, None)
    v_ref[...] = proj(2)

    cb = proj(3)
    u = proj(4) * proj(5)
    if tiles_per_seq > 1:
        @pl.when(i % tiles_per_seq == 0)
        def _():
            carry_ref[0:2, :] = st_ref[0]
    for s in range(n_seq):
        rows = slice(s * seq_rows, (s + 1) * seq_rows)
        us = u[rows]
        if tiles_per_seq > 1:
            hist0, hist1 = carry_ref[0:1, :], carry_ref[1:2, :]
        else:
            hist0, hist1 = st_ref[s, 0:1, :], st_ref[s, 1:2, :]
        row = lax.broadcasted_iota(jnp.int32, us.shape, 0)
        back1 = pltpu.roll(us, 1, axis=0)
        back2 = pltpu.roll(us, 2, axis=0)
        u1 = jnp.where(row == 0, hist1, back1)
        u2 = jnp.where(row == 0, hist0, jnp.where(row == 1, hist1, back2))
        conv = cw_ref[0:1, :] * u2 + cw_ref[1:2, :] * u1 + cw_ref[2:3, :] * us
        co_ref[rows, :] = (cb[rows] * conv).astype(co_ref.dtype)
        last = back2[0:2, :]
        sto_ref[s] = last
        if tiles_per_seq > 1:
            carry_ref[0:2, :] = last


def _inproj(x, norm_w, w_in, rope_tabs, conv_w, conv_state, seq_len, n_heads, q_scale):
    t, d = x.shape
    width = w_in.shape[1] // 6
    n_batch = conv_state.shape[0]
    hist = conv_state.shape[1]
    assert hist == 2 and seq_len >= hist and t == n_batch * seq_len
    tm = INPROJ_TOKEN_TILE
    if seq_len >= tm:
        assert seq_len % tm == 0
        n_seq, tiles_per_seq = 1, seq_len // tm
    else:
        assert tm % seq_len == 0 and seq_len % 8 == 0
        n_seq, tiles_per_seq = tm // seq_len, 1
    cos, sin_lo, sin_hi = rope_tabs
    pos_tiles = cos.shape[0] // tm
    tab_spec = pl.BlockSpec((tm, V7X_LANES), lambda i: (i % pos_tiles, 0))
    tok = lambda i: (i, 0)
    const = lambda i: (0, 0)
    st_spec = pl.BlockSpec((n_seq, hist, width), lambda i: (i // tiles_per_seq, 0, 0))
    vmem = _vmem_limit(
        pipelined=[((tm, d), F32), ((tm, width), BF16), ((tm, width), F32), ((tm, width), F32),
                   ((tm, width), BF16), ((tm, V7X_LANES), F32), ((tm, V7X_LANES), F32),
                   ((tm, V7X_LANES), F32)],
        resident=[(w_in.shape, BF16), ((tm, d), BF16)] + [((tm, width), F32)] * 8)
    return pl.pallas_call(
        functools.partial(_inproj_kernel, width=width, n_heads=n_heads, n_seq=n_seq,
                          tiles_per_seq=tiles_per_seq, q_scale=q_scale),
        name="inproj",
        grid=(t // tm,),
        in_specs=[
            pl.BlockSpec((tm, d), tok),
            pl.BlockSpec((1, d), const),
            pl.BlockSpec(w_in.shape, const, pipeline_mode=pl.Buffered(1)),
            tab_spec, tab_spec, tab_spec,
            pl.BlockSpec(conv_w.shape, const),
            st_spec,
        ],
        out_specs=[
            pl.BlockSpec((tm, width), tok),
            pl.BlockSpec((tm, width), tok),
            pl.BlockSpec((tm, width), tok),
            pl.BlockSpec((tm, width), tok),
            st_spec,
        ],
        out_shape=[
            jax.ShapeDtypeStruct((t, width), BF16),
            jax.ShapeDtypeStruct((t, width), F32),
            jax.ShapeDtypeStruct((t, width), F32),
            jax.ShapeDtypeStruct((t, width), BF16),
            jax.ShapeDtypeStruct(conv_state.shape, F32),
        ],
        scratch_shapes=[pltpu.VMEM((8, width), F32)],
        compiler_params=pltpu.CompilerParams(
            dimension_semantics=("arbitrary",), vmem_limit_bytes=vmem),
    )(x, norm_w, w_in, cos, sin_lo, sin_hi, conv_w, conv_state)


def _stack_components(q):
    lane = lax.broadcasted_iota(jnp.int32, q.shape, 1)
    half = q.shape[1] // 2
    zero = jnp.zeros_like(q)
    return jnp.concatenate([jnp.where(lane < half, q, zero), jnp.where(lane >= half, q, zero)], axis=0)


def _visible(shape, q_start, k_start):
    assert CHUNK & (CHUNK - 1) == 0
    rows = shape[1] // 2
    c = lax.broadcasted_iota(jnp.int32, shape, 1)
    q_pos = q_start + jnp.where(c >= rows, c - rows, c)
    k_pos = k_start + lax.broadcasted_iota(jnp.int32, shape, 0)
    return k_pos < (q_pos | (CHUNK - 1)) + 1


def _scores(k, q2):
    return lax.dot_general(k, q2, (((1,), (1,)), ((), ())), preferred_element_type=F32)


def _weighted_values(v, p):
    return lax.dot_general(v, p, (((0,), (0,)), ((), ())), preferred_element_type=F32)


def _lambda(lq1_ref, lk1_ref, lq2_ref, lk2_ref, lambda_init):
    s1 = jnp.sum(lq1_ref[...] * lk1_ref[...], axis=1, keepdims=True)
    s2 = jnp.sum(lq2_ref[...] * lk2_ref[...], axis=1, keepdims=True)
    return jnp.exp(s1) - jnp.exp(s2) + lambda_init


def _diff_finish(acc, l, lam, gain_col, lambda_init):
    rows = acc.shape[1] // 2
    o = acc[:, :rows] / l[:, :rows] - lam * (acc[:, rows:] / l[:, rows:])
    y = (o * lax.rsqrt(jnp.mean(o * o, axis=0, keepdims=True) + EPS)) * gain_col
    return (y * (1.0 - lambda_init)).T


def _attn_prompt_kernel(q_ref, k_ref, v_ref, lq1_ref, lk1_ref, lq2_ref, lk2_ref, g_ref, o_ref,
                        m_ref, l_ref, acc_ref, *, lambda_init):
    i = pl.program_id(2)
    blk = q_ref.shape[0]
    q2 = _stack_components(q_ref[...])
    m_ref[...] = jnp.full_like(m_ref, MASK_VALUE)
    l_ref[...] = jnp.zeros_like(l_ref)
    acc_ref[...] = jnp.zeros_like(acc_ref)

    def step(kb, masked):
        start = pl.multiple_of(kb * blk, blk)
        k = k_ref[pl.ds(start, blk), :].astype(BF16)
        v = v_ref[pl.ds(start, blk), :].astype(BF16)
        s = _scores(k, q2)
        if masked:
            s = jnp.where(_visible(s.shape, 0, 0), s, MASK_VALUE)
        m_prev = m_ref[...]
        m_new = jnp.maximum(m_prev, jnp.max(s, axis=0, keepdims=True))
        alpha = jnp.exp(m_prev - m_new)
        p = jnp.exp(s - m_new)
        l_ref[...] = alpha * l_ref[...] + jnp.sum(p, axis=0, keepdims=True)
        acc_ref[...] = alpha * acc_ref[...] + _weighted_values(v, p.astype(BF16))
        m_ref[...] = m_new

    def body(kb, carry):
        step(kb, False)
        return carry

    lax.fori_loop(0, i, body, 0)
    step(i, True)

    lam = _lambda(lq1_ref, lk1_ref, lq2_ref, lk2_ref, lambda_init)
    o_ref[...] = _diff_finish(acc_ref[...], l_ref[...], lam, g_ref[...], lambda_init).astype(o_ref.dtype)


def _attn_prompt(q, k, v, lams, gain_col, n_batch, seq_len, n_heads, lambda_init):
    t, width = q.shape
    e = width // n_heads
    blk = ATTN_BLOCK
    assert seq_len % blk == 0 and blk % CHUNK == 0 and e == V7X_LANES
    nq = seq_len // blk
    lam_spec = pl.BlockSpec(lams[0].shape, lambda b, h, i: (0, 0))
    vmem = _vmem_limit(
        pipelined=[((blk, e), BF16), ((seq_len, e), F32), ((seq_len, e), F32), ((blk, e), BF16)],
        resident=[((blk, 2 * blk), F32)] * 4 + [((e, 2 * blk), F32)] * 4)
    return pl.pallas_call(
        functools.partial(_attn_prompt_kernel, lambda_init=lambda_init),
        name="attn_prompt",
        grid=(n_batch, n_heads, nq),
        in_specs=[
            pl.BlockSpec((blk, e), lambda b, h, i: (b * nq + i, h)),
            pl.BlockSpec((seq_len, e), lambda b, h, i: (b, h)),
            pl.BlockSpec((seq_len, e), lambda b, h, i: (b, h)),
            lam_spec, lam_spec, lam_spec, lam_spec,
            pl.BlockSpec(gain_col.shape, lambda b, h, i: (0, 0)),
        ],
        out_specs=pl.BlockSpec((blk, e), lambda b, h, i: (b * nq + i, h)),
        out_shape=jax.ShapeDtypeStruct((t, width), BF16),
        scratch_shapes=[pltpu.VMEM((1, 2 * blk), F32), pltpu.VMEM((1, 2 * blk), F32),
                        pltpu.VMEM((e, 2 * blk), F32)],
        compiler_params=pltpu.CompilerParams(
            dimension_semantics=("parallel", "parallel", "arbitrary"), vmem_limit_bytes=vmem),
    )(q, k, v, *lams, gain_col)


def _attn_sample_kernel(q_ref, kn_ref, vn_ref, kc_ref, vc_ref, lq1_ref, lk1_ref, lq2_ref, lk2_ref,
                        g_ref, o_ref, *, n_heads, past_len, lambda_init):
    rows = q_ref.shape[0]
    e = q_ref.shape[1] // n_heads
    lam = _lambda(lq1_ref, lk1_ref, lq2_ref, lk2_ref, lambda_init)
    gain_col = g_ref[...]
    visible = _visible((rows, 2 * rows), past_len, past_len)
    for hd in range(n_heads):
        cols = slice(hd * e, (hd + 1) * e)
        kc = kc_ref[pl.ds(hd, past_len, stride=n_heads), :].astype(BF16)
        vc = vc_ref[pl.ds(hd, past_len, stride=n_heads), :].astype(BF16)
        q2 = _stack_components(q_ref[:, cols])
        s_c = _scores(kc, q2)
        s_n = jnp.where(visible, _scores(kn_ref[:, cols].astype(BF16), q2), MASK_VALUE)
        m = jnp.maximum(jnp.max(s_c, axis=0, keepdims=True), jnp.max(s_n, axis=0, keepdims=True))
        p_c = jnp.exp(s_c - m)
        p_n = jnp.exp(s_n - m)
        l = jnp.sum(p_c, axis=0, keepdims=True) + jnp.sum(p_n, axis=0, keepdims=True)
        acc = (_weighted_values(vc, p_c.astype(BF16))
               + _weighted_values(vn_ref[:, cols].astype(BF16), p_n.astype(BF16)))
        o_ref[:, cols] = _diff_finish(acc, l, lam, gain_col, lambda_init).astype(o_ref.dtype)


def _attn_sample(q, k_new, v_new, k_cache, v_cache, lams, gain_col, n_batch, seq_len, n_heads,
                 lambda_init):
    t, width = q.shape
    e = width // n_heads
    past_len = k_cache.shape[0] // (n_batch * n_heads)
    assert seq_len % 8 == 0 and past_len % 8 == 0 and k_cache.shape[1] == e
    tok = lambda b: (b, 0)
    lam_spec = pl.BlockSpec(lams[0].shape, lambda b: (0, 0))
    vmem = _vmem_limit(
        pipelined=[((past_len * n_heads, e), F32)] * 2 + [((seq_len, width), F32)] * 4,
        resident=[((past_len, 2 * seq_len), F32)] * 4 + [((past_len, e), F32)] * 4)
    return pl.pallas_call(
        functools.partial(_attn_sample_kernel, n_heads=n_heads, past_len=past_len,
                          lambda_init=lambda_init),
        name="attn_sample",
        grid=(n_batch,),
        in_specs=[
            pl.BlockSpec((seq_len, width), tok),
            pl.BlockSpec((seq_len, width), tok),
            pl.BlockSpec((seq_len, width), tok),
            pl.BlockSpec((past_len * n_heads, e), tok),
            pl.BlockSpec((past_len * n_heads, e), tok),
            lam_spec, lam_spec, lam_spec, lam_spec,
            pl.BlockSpec(gain_col.shape, lambda b: (0, 0)),
        ],
        out_specs=pl.BlockSpec((seq_len, width), tok),
        out_shape=jax.ShapeDtypeStruct((t, width), BF16),
        compiler_params=pltpu.CompilerParams(
            dimension_semantics=("parallel",), vmem_limit_bytes=vmem),
    )(q, k_new, v_new, k_cache, v_cache, *lams, gain_col)


def _outproj_kernel(x_ref, a_ref, c_ref, w_ref, o_ref):
    mixed = jnp.concatenate([a_ref[...], c_ref[...]], axis=1)
    o_ref[...] = x_ref[...] + jnp.dot(mixed, w_ref[...], preferred_element_type=F32)


def _outproj(x, attn_out, conv_out, w_out):
    t, d = x.shape
    wa, wc = attn_out.shape[1], conv_out.shape[1]
    tm = min(OUTPROJ_TOKEN_TILE, t)
    assert t % tm == 0 and wa + wc == w_out.shape[0]
    tok = lambda i: (i, 0)
    vmem = _vmem_limit(
        pipelined=[((tm, d), F32), ((tm, wa), BF16), ((tm, wc), BF16), (w_out.shape, BF16),
                   ((tm, d), F32)],
        resident=[((tm, wa + wc), BF16), ((tm, d), F32)])
    return pl.pallas_call(
        _outproj_kernel,
        name="outproj",
        grid=(t // tm,),
        in_specs=[
            pl.BlockSpec((tm, d), tok),
            pl.BlockSpec((tm, wa), tok),
            pl.BlockSpec((tm, wc), tok),
            pl.BlockSpec(w_out.shape, lambda i: (0, 0)),
        ],
        out_specs=pl.BlockSpec((tm, d), tok),
        out_shape=jax.ShapeDtypeStruct((t, d), F32),
        compiler_params=pltpu.CompilerParams(
            dimension_semantics=("parallel",), vmem_limit_bytes=vmem),
    )(x, attn_out, conv_out, w_out)


def _rope_tables(pos, head_dim, rows):
    half = head_dim // 2
    inv = ROPE_THETA ** (-jnp.arange(0, head_dim, 2, dtype=F32) / head_dim)
    ang = pos.astype(F32)[:, None] * inv[None, :]
    cos, sin = jnp.cos(ang), jnp.sin(ang)
    zero = jnp.zeros_like(sin)
    reps = V7X_LANES // head_dim
    cos_t = jnp.tile(jnp.concatenate([cos, cos], axis=1), (rows // pos.shape[0], reps))
    sin_lo = jnp.tile(jnp.concatenate([-sin, zero], axis=1), (rows // pos.shape[0], reps))
    sin_hi = jnp.tile(jnp.concatenate([zero, sin], axis=1), (rows // pos.shape[0], reps))
    return cos_t, sin_lo, sin_hi


def kernel(x_prompt, x_sample, cache_k, cache_v, state_conv, ffn1_norm, ffn1_w_gate, ffn1_w_up,
           ffn1_w_down, mix_norm, w_in, lambda_q1, lambda_k1, lambda_q2, lambda_k2, subln_gain,
           conv_w, w_out, ffn2_norm, ffn2_w_gate, ffn2_w_up, ffn2_w_down, final_norm):
    n_bp, seq_p, d = x_prompt.shape
    n_bs, seq_s, _ = x_sample.shape
    depth, _, past_len, n_heads, key_dim = cache_k.shape
    value_dim = cache_v.shape[-1]
    head_dim = key_dim // 2
    conv_ch = conv_w.shape[-1]
    attn_width = n_heads * key_dim
    assert key_dim == V7X_LANES and value_dim == V7X_LANES and attn_width == conv_ch
    q_scale = head_dim ** -0.5

    tabs_p = _rope_tables(jnp.arange(seq_p), head_dim, seq_p)
    tabs_s = _rope_tables(past_len + jnp.arange(seq_s), head_dim, max(seq_s, INPROJ_TOKEN_TILE))

    xp = x_prompt.reshape(n_bp * seq_p, d)
    xs = x_sample.reshape(n_bs * seq_s, d)
    row = lambda a: a.reshape(1, -1)
    outs = {name: [] for name in ("kp", "vp", "cp", "ks", "vs", "cs")}
    for l in range(depth):
        lambda_init = 0.8 - 0.6 * math.exp(-0.3 * l)
        last = l == depth - 1
        w1g, w1u, w1d = (w[l].astype(BF16) for w in (ffn1_w_gate, ffn1_w_up, ffn1_w_down))
        w2g, w2u, w2d = (w[l].astype(BF16) for w in (ffn2_w_gate, ffn2_w_up, ffn2_w_down))
        w_in_l, w_out_l = w_in[l].astype(BF16), w_out[l].astype(BF16)
        lams = tuple(row(a[l]) for a in (lambda_q1, lambda_k1, lambda_q2, lambda_k2))
        gain = subln_gain[l].reshape(-1, 1)
        fin = row(final_norm)
        zeros_state = jnp.zeros((n_bp, conv_w.shape[1] - 1, conv_ch), F32)

        def layer(x, tabs, state, seq_len, attend):
            x = _ffn(x, row(ffn1_norm[l]), w1g, w1u, w1d, fin, False)
            q, k, v, conv_out, new_state = _inproj(
                x, row(mix_norm[l]), w_in_l, tabs, conv_w[l], state, seq_len, n_heads, q_scale)
            attn_out = attend(q, k, v)
            x = _outproj(x, attn_out, conv_out, w_out_l)
            x = _ffn(x, row(ffn2_norm[l]), w2g, w2u, w2d, fin, last)
            return x, k, v, new_state

        xp, kp, vp, cp = layer(
            xp, tabs_p, zeros_state, seq_p,
            lambda q, k, v: _attn_prompt(q, k, v, lams, gain, n_bp, seq_p, n_heads, lambda_init))
        kc = cache_k[l].reshape(n_bs * past_len * n_heads, key_dim)
        vc = cache_v[l].reshape(n_bs * past_len * n_heads, value_dim)
        xs, ks, vs, cs = layer(
            xs, tabs_s, state_conv[l], seq_s,
            lambda q, k, v: _attn_sample(q, k, v, kc, vc, lams, gain, n_bs, seq_s, n_heads,
                                         lambda_init))
        outs["kp"].append(kp.reshape(n_bp, seq_p, n_heads, key_dim))
        outs["vp"].append(vp.reshape(n_bp, seq_p, n_heads, value_dim))
        outs["cp"].append(cp)
        outs["ks"].append(ks.reshape(n_bs, seq_s, n_heads, key_dim))
        outs["vs"].append(vs.reshape(n_bs, seq_s, n_heads, value_dim))
        outs["cs"].append(cs)
    return (xp.reshape(x_prompt.shape), xs.reshape(x_sample.shape),
            jnp.stack(outs["kp"]), jnp.stack(outs["vp"]), jnp.stack(outs["cp"]),
            jnp.stack(outs["ks"]), jnp.stack(outs["vs"]), jnp.stack(outs["cs"]))
```

```python
import functools
import math

import jax
import jax.numpy as jnp
from jax import lax
from jax.experimental import pallas as pl
from jax.experimental.pallas import tpu as pltpu

F32 = jnp.float32
BF16 = jnp.bfloat16

CHUNK = 64
ROPE_THETA = 10000.0
EPS = 1e-6
MASK_VALUE = -1e30

V7X_LANES = 128
V7X_VMEM_BYTES = 64 * 1024 * 1024
VMEM_BUDGET_BYTES = V7X_VMEM_BYTES * 7 // 8

CAST_BLOCK_BYTES = 6 * 1024 * 1024
FFN_TOKEN_TILE = 512
FFN_COL_TILE = 512
INPROJ_TOKEN_TILE = 256
OUTPROJ_TOKEN_TILE = 512
ATTN_BLOCK = 512
ONES_ROWS = 16
assert CHUNK & (CHUNK - 1) == 0


def _nbytes(shape, dtype):
    return math.prod(shape) * jnp.dtype(dtype).itemsize


def _vmem_limit(pipelined, resident):
    need = 2 * sum(_nbytes(s, d) for s, d in pipelined) + sum(_nbytes(s, d) for s, d in resident)
    return int(min(VMEM_BUDGET_BYTES, max(need, 16 * 1024 * 1024)))


def _rms(x, gain):
    return (x * lax.rsqrt(jnp.mean(x * x, axis=-1, keepdims=True) + EPS)) * gain


def _cast_kernel(x_ref, o_ref):
    o_ref[...] = x_ref[...].astype(o_ref.dtype)


def _to_bf16(w):
    r, c = w.shape
    slab = r
    while slab % 32 == 0 and _nbytes((slab, c), w.dtype) > CAST_BLOCK_BYTES:
        slab //= 2
    return pl.pallas_call(
        _cast_kernel,
        name="cast",
        grid=(r // slab,),
        in_specs=[pl.BlockSpec((slab, c), lambda i: (i, 0))],
        out_specs=pl.BlockSpec((slab, c), lambda i: (i, 0)),
        out_shape=jax.ShapeDtypeStruct(w.shape, BF16),
        compiler_params=pltpu.CompilerParams(
            dimension_semantics=("parallel",),
            vmem_limit_bytes=_vmem_limit([((slab, c), w.dtype), ((slab, c), BF16)], [])),
    )(w)


def _ffn_kernel(x_ref, nw_ref, wg_ref, wu_ref, wd_ref, fn_ref, o_ref, h_ref, *, apply_final_norm):
    f = pl.program_id(1)

    @pl.when(f == 0)
    def _():
        h_ref[...] = _rms(x_ref[...], nw_ref[...]).astype(BF16)
        o_ref[...] = jnp.zeros_like(o_ref)

    h = h_ref[...]
    gate = jnp.dot(h, wg_ref[...], preferred_element_type=F32)
    up = jnp.dot(h, wu_ref[...], preferred_element_type=F32)
    act = ((gate * jax.nn.sigmoid(gate)) * up).astype(BF16)
    o_ref[...] += jnp.dot(act, wd_ref[...], preferred_element_type=F32)

    @pl.when(f == pl.num_programs(1) - 1)
    def _():
        y = x_ref[...] + 0.5 * o_ref[...]
        if apply_final_norm:
            y = _rms(y, fn_ref[...])
        o_ref[...] = y


def _ffn(x, norm_w, w_gate, w_up, w_down, final_w, apply_final_norm):
    t, d = x.shape
    ffn = w_gate.shape[1]
    tm, tf = min(FFN_TOKEN_TILE, t), FFN_COL_TILE
    assert t % tm == 0 and ffn % tf == 0
    vmem = _vmem_limit(
        pipelined=[((tm, d), F32), ((tm, d), F32), ((d, tf), BF16), ((d, tf), BF16), ((tf, d), BF16)],
        resident=[((tm, d), BF16), ((tm, tf), F32), ((tm, tf), F32), ((tm, tf), F32), ((tm, d), F32)])
    return pl.pallas_call(
        functools.partial(_ffn_kernel, apply_final_norm=apply_final_norm),
        name="ffn",
        grid=(t // tm, ffn // tf),
        in_specs=[
            pl.BlockSpec((tm, d), lambda i, f: (i, 0)),
            pl.BlockSpec((1, d), lambda i, f: (0, 0)),
            pl.BlockSpec((d, tf), lambda i, f: (0, f)),
            pl.BlockSpec((d, tf), lambda i, f: (0, f)),
            pl.BlockSpec((tf, d), lambda i, f: (f, 0)),
            pl.BlockSpec((1, d), lambda i, f: (0, 0)),
        ],
        out_specs=pl.BlockSpec((tm, d), lambda i, f: (i, 0)),
        out_shape=jax.ShapeDtypeStruct((t, d), F32),
        scratch_shapes=[pltpu.VMEM((tm, d), BF16)],
        compiler_params=pltpu.CompilerParams(
            dimension_semantics=("parallel", "arbitrary"), vmem_limit_bytes=vmem),
    )(x, norm_w, w_gate, w_up, w_down, final_w)


def _inproj_kernel(x_ref, nw_ref, w_ref, cos_ref, sin_lo_ref, sin_hi_ref, cw_ref, st_ref,
                   q_ref, k_ref, v_ref, co_ref, sto_ref, carry_ref,
                   *, width, n_heads, n_seq, tiles_per_seq, q_scale):
    i = pl.program_id(0)
    tm = x_ref.shape[0]
    seq_rows = tm // n_seq
    h = _rms(x_ref[...], nw_ref[...]).astype(BF16)

    def proj(col):
        return jnp.dot(h, w_ref[:, col * width:(col + 1) * width], preferred_element_type=F32)

    cos, sin_lo, sin_hi = cos_ref[...], sin_lo_ref[...], sin_hi_ref[...]

    def rope_store(x, out_ref, scale):
        for hd in range(n_heads):
            xh = x[:, hd * V7X_LANES:(hd + 1) * V7X_LANES]
            r = (xh * cos + pltpu.roll(xh, V7X_LANES - 32, axis=1) * sin_lo
                 + pltpu.roll(xh, 32, axis=1) * sin_hi)
            if scale is not None:
                r = r * scale
            out_ref[:, hd * V7X_LANES:(hd + 1) * V7X_LANES] = r.astype(out_ref.dtype)

    rope_store(proj(0), q_ref, q_scale)
    rope_store(proj(1), k_ref, None)
    v_ref[...] = proj(2)

    cb = proj(3)
    u = proj(4) * proj(5)
    if tiles_per_seq > 1:
        @pl.when(i % tiles_per_seq == 0)
        def _():
            carry_ref[0:2, :] = st_ref[0]
    for s in range(n_seq):
        rows = slice(s * seq_rows, (s + 1) * seq_rows)
        us = u[rows]
        if tiles_per_seq > 1:
            hist0, hist1 = carry_ref[0:1, :], carry_ref[1:2, :]
        else:
            hist0, hist1 = st_ref[s, 0:1, :], st_ref[s, 1:2, :]
        row = lax.broadcasted_iota(jnp.int32, us.shape, 0)
        back1 = pltpu.roll(us, 1, axis=0)
        back2 = pltpu.roll(us, 2, axis=0)
        u1 = jnp.where(row == 0, hist1, back1)
        u2 = jnp.where(row == 0, hist0, jnp.where(row == 1, hist1, back2))
        conv = cw_ref[0:1, :] * u2 + cw_ref[1:2, :] * u1 + cw_ref[2:3, :] * us
        co_ref[rows, :] = (cb[rows] * conv).astype(co_ref.dtype)
        last = back2[0:2, :]
        sto_ref[s] = last
        if tiles_per_seq > 1:
            carry_ref[0:2, :] = last


def _inproj(x, norm_w, w_in, rope_tabs, conv_w, conv_state, seq_len, n_heads, q_scale):
    t, d = x.shape
    width = w_in.shape[1] // 6
    n_batch = conv_state.shape[0]
    hist = conv_state.shape[1]
    assert hist == 2 and seq_len >= hist and t == n_batch * seq_len
    tm = INPROJ_TOKEN_TILE
    if seq_len >= tm:
        assert seq_len % tm == 0
        n_seq, tiles_per_seq = 1, seq_len // tm
    else:
        assert tm % seq_len == 0 and seq_len % 8 == 0
        n_seq, tiles_per_seq = tm // seq_len, 1
    cos, sin_lo, sin_hi = rope_tabs
    pos_tiles = cos.shape[0] // tm
    tab_spec = pl.BlockSpec((tm, V7X_LANES), lambda i: (i % pos_tiles, 0))
    tok = lambda i: (i, 0)
    const = lambda i: (0, 0)
    st_spec = pl.BlockSpec((n_seq, hist, width), lambda i: (i // tiles_per_seq, 0, 0))
    vmem = _vmem_limit(
        pipelined=[((tm, d), F32), ((tm, width), BF16), ((tm, width), F32), ((tm, width), F32),
                   ((tm, width), BF16), ((tm, V7X_LANES), F32), ((tm, V7X_LANES), F32),
                   ((tm, V7X_LANES), F32)],
        resident=[(w_in.shape, BF16), ((tm, d), BF16)] + [((tm, width), F32)] * 8)
    return pl.pallas_call(
        functools.partial(_inproj_kernel, width=width, n_heads=n_heads, n_seq=n_seq,
                          tiles_per_seq=tiles_per_seq, q_scale=q_scale),
        name="inproj",
        grid=(t // tm,),
        in_specs=[
            pl.BlockSpec((tm, d), tok),
            pl.BlockSpec((1, d), const),
            pl.BlockSpec(w_in.shape, const, pipeline_mode=pl.Buffered(1)),
            tab_spec, tab_spec, tab_spec,
            pl.BlockSpec(conv_w.shape, const),
            st_spec,
        ],
        out_specs=[
            pl.BlockSpec((tm, width), tok),
            pl.BlockSpec((tm, width), tok),
            pl.BlockSpec((tm, width), tok),
            pl.BlockSpec((tm, width), tok),
            st_spec,
        ],
        out_shape=[
            jax.ShapeDtypeStruct((t, width), BF16),
            jax.ShapeDtypeStruct((t, width), F32),
            jax.ShapeDtypeStruct((t, width), F32),
            jax.ShapeDtypeStruct((t, width), BF16),
            jax.ShapeDtypeStruct(conv_state.shape, F32),
        ],
        scratch_shapes=[pltpu.VMEM((8, width), F32)],
        compiler_params=pltpu.CompilerParams(
            dimension_semantics=("arbitrary",), vmem_limit_bytes=vmem),
    )(x, norm_w, w_in, cos, sin_lo, sin_hi, conv_w, conv_state)


def _stack_components(q):
    lane = lax.broadcasted_iota(jnp.int32, q.shape, 1)
    half = q.shape[1] // 2
    zero = jnp.zeros_like(q)
    return jnp.concatenate([jnp.where(lane < half, q, zero), jnp.where(lane >= half, q, zero)], axis=0)


def _visible(shape, q_start, k_start):
    assert CHUNK & (CHUNK - 1) == 0
    rows = shape[1] // 2
    c = lax.broadcasted_iota(jnp.int32, shape, 1)
    q_pos = q_start + jnp.where(c >= rows, c - rows, c)
    k_pos = k_start + lax.broadcasted_iota(jnp.int32, shape, 0)
    return k_pos < (q_pos | (CHUNK - 1)) + 1


def _scores(k, q2):
    return lax.dot_general(k, q2, (((1,), (1,)), ((), ())), preferred_element_type=F32)


def _weighted_values(v, p):
    return lax.dot_general(v, p, (((0,), (0,)), ((), ())), preferred_element_type=F32)


def _lambda(lq1_ref, lk1_ref, lq2_ref, lk2_ref, lambda_init):
    s1 = jnp.sum(lq1_ref[...] * lk1_ref[...], axis=1, keepdims=True)
    s2 = jnp.sum(lq2_ref[...] * lk2_ref[...], axis=1, keepdims=True)
    return jnp.exp(s1) - jnp.exp(s2) + lambda_init


def _diff_finish(acc, l, lam, gain_col, lambda_init):
    rows = acc.shape[1] // 2
    o = acc[:, :rows] / l[:, :rows] - lam * (acc[:, rows:] / l[:, rows:])
    y = (o * lax.rsqrt(jnp.mean(o * o, axis=0, keepdims=True) + EPS)) * gain_col
    return (y * (1.0 - lambda_init)).T


def _attn_prompt_kernel(q_ref, k_ref, v_ref, lq1_ref, lk1_ref, lq2_ref, lk2_ref, g_ref, o_ref,
                        kb_ref, vt_ref, q2_ref, s_ref, bias_ref, m_ref, acc_ref, *, lambda_init):
    i = pl.program_id(2)
    blk = q_ref.shape[0]
    e = v_ref.shape[1]
    n_blocks = kb_ref.shape[0]

    @pl.when(i == 0)
    def _():
        bias_ref[...] = jnp.where(_visible(bias_ref.shape, 0, 0), 0.0, MASK_VALUE)
        ones = jnp.ones((ONES_ROWS, blk), BF16)
        for j in range(n_blocks):
            rows = slice(j * blk, (j + 1) * blk)
            kb_ref[j] = k_ref[rows, :].astype(BF16)
            vt_ref[j] = jnp.concatenate([v_ref[rows, :].T.astype(BF16), ones], axis=0)

    q2_ref[...] = _stack_components(q_ref[...])
    m_ref[...] = jnp.full_like(m_ref, MASK_VALUE)
    acc_ref[...] = jnp.zeros_like(acc_ref)

    def scores(kb, slot):
        s_ref[slot] = _scores(kb_ref[kb], q2_ref[...])

    def accumulate(kb, slot, diagonal):
        s = s_ref[slot]
        if diagonal:
            s = s + bias_ref[...]
        m_prev = m_ref[...]
        m_new = jnp.maximum(m_prev, jnp.max(s, axis=0, keepdims=True))
        alpha = jnp.exp(m_prev - m_new)
        p = jnp.exp(s - m_new).astype(BF16)
        acc_ref[...] = alpha * acc_ref[...] + jnp.dot(vt_ref[kb], p, preferred_element_type=F32)
        m_ref[...] = m_new

    scores(0, 0)

    def pair(j, carry):
        scores(2 * j + 1, 1)
        accumulate(2 * j, 0, False)
        scores(2 * j + 2, 0)
        accumulate(2 * j + 1, 1, False)
        return carry

    lax.fori_loop(0, i // 2, pair, 0)

    @pl.when(i % 2 == 1)
    def _():
        scores(i, 1)
        accumulate(i - 1, 0, False)

    accumulate(i, i % 2, True)

    lam = _lambda(lq1_ref, lk1_ref, lq2_ref, lk2_ref, lambda_init)
    o_ref[...] = _diff_finish(acc_ref[0:e, :], acc_ref[e:e + 1, :], lam, g_ref[...],
                              lambda_init).astype(o_ref.dtype)


def _attn_prompt(q, k, v, lams, gain_col, n_batch, seq_len, n_heads, lambda_init):
    t, width = q.shape
    e = width // n_heads
    blk = ATTN_BLOCK
    assert seq_len % blk == 0 and blk % CHUNK == 0 and e == V7X_LANES
    nq = seq_len // blk
    lam_spec = pl.BlockSpec(lams[0].shape, lambda b, h, i: (0, 0))
    n_blocks = seq_len // blk
    scratch = [((n_blocks, blk, e), BF16), ((n_blocks, e + ONES_ROWS, blk), BF16), ((2 * blk, e), BF16),
               ((2, blk, 2 * blk), F32), ((blk, 2 * blk), F32), ((1, 2 * blk), F32),
               ((e + ONES_ROWS, 2 * blk), F32)]
    vmem = _vmem_limit(
        pipelined=[((blk, e), BF16), ((seq_len, e), F32), ((seq_len, e), F32), ((blk, e), BF16)],
        resident=scratch + [((blk, 2 * blk), F32)] * 6)
    return pl.pallas_call(
        functools.partial(_attn_prompt_kernel, lambda_init=lambda_init),
        name="attn_prompt",
        grid=(n_batch, n_heads, nq),
        in_specs=[
            pl.BlockSpec((blk, e), lambda b, h, i: (b * nq + i, h)),
            pl.BlockSpec((seq_len, e), lambda b, h, i: (b, h)),
            pl.BlockSpec((seq_len, e), lambda b, h, i: (b, h)),
            lam_spec, lam_spec, lam_spec, lam_spec,
            pl.BlockSpec(gain_col.shape, lambda b, h, i: (0, 0)),
        ],
        out_specs=pl.BlockSpec((blk, e), lambda b, h, i: (b * nq + i, h)),
        out_shape=jax.ShapeDtypeStruct((t, width), BF16),
        scratch_shapes=[pltpu.VMEM(s, d) for s, d in scratch],
        compiler_params=pltpu.CompilerParams(
            dimension_semantics=("parallel", "parallel", "arbitrary"), vmem_limit_bytes=vmem),
    )(q, k, v, *lams, gain_col)


def _attn_sample_kernel(q_ref, kn_ref, vn_ref, kc_ref, vc_ref, lq1_ref, lk1_ref, lq2_ref, lk2_ref,
                        g_ref, o_ref, *, n_heads, past_len, lambda_init):
    rows = q_ref.shape[0]
    e = q_ref.shape[1] // n_heads
    lam = _lambda(lq1_ref, lk1_ref, lq2_ref, lk2_ref, lambda_init)
    gain_col = g_ref[...]
    visible = _visible((rows, 2 * rows), past_len, past_len)
    for hd in range(n_heads):
        cols = slice(hd * e, (hd + 1) * e)
        kc = kc_ref[pl.ds(hd, past_len, stride=n_heads), :].astype(BF16)
        vc = vc_ref[pl.ds(hd, past_len, stride=n_heads), :].astype(BF16)
        q2 = _stack_components(q_ref[:, cols])
        s_c = _scores(kc, q2)
        s_n = jnp.where(visible, _scores(kn_ref[:, cols].astype(BF16), q2), MASK_VALUE)
        m = jnp.maximum(jnp.max(s_c, axis=0, keepdims=True), jnp.max(s_n, axis=0, keepdims=True))
        p_c = jnp.exp(s_c - m)
        p_n = jnp.exp(s_n - m)
        l = jnp.sum(p_c, axis=0, keepdims=True) + jnp.sum(p_n, axis=0, keepdims=True)
        acc = (_weighted_values(vc, p_c.astype(BF16))
               + _weighted_values(vn_ref[:, cols].astype(BF16), p_n.astype(BF16)))
        o_ref[:, cols] = _diff_finish(acc, l, lam, gain_col, lambda_init).astype(o_ref.dtype)


def _attn_sample(q, k_new, v_new, k_cache, v_cache, lams, gain_col, n_batch, seq_len, n_heads,
                 lambda_init):
    t, width = q.shape
    e = width // n_heads
    past_len = k_cache.shape[0] // (n_batch * n_heads)
    assert seq_len % 8 == 0 and past_len % 8 == 0 and k_cache.shape[1] == e
    tok = lambda b: (b, 0)
    lam_spec = pl.BlockSpec(lams[0].shape, lambda b: (0, 0))
    vmem = _vmem_limit(
        pipelined=[((past_len * n_heads, e), F32)] * 2 + [((seq_len, width), F32)] * 4,
        resident=[((past_len, 2 * seq_len), F32)] * 4 + [((past_len, e), F32)] * 4)
    return pl.pallas_call(
        functools.partial(_attn_sample_kernel, n_heads=n_heads, past_len=past_len,
                          lambda_init=lambda_init),
        name="attn_sample",
        grid=(n_batch,),
        in_specs=[
            pl.BlockSpec((seq_len, width), tok),
            pl.BlockSpec((seq_len, width), tok),
            pl.BlockSpec((seq_len, width), tok),
            pl.BlockSpec((past_len * n_heads, e), tok),
            pl.BlockSpec((past_len * n_heads, e), tok),
            lam_spec, lam_spec, lam_spec, lam_spec,
            pl.BlockSpec(gain_col.shape, lambda b: (0, 0)),
        ],
        out_specs=pl.BlockSpec((seq_len, width), tok),
        out_shape=jax.ShapeDtypeStruct((t, width), BF16),
        compiler_params=pltpu.CompilerParams(
            dimension_semantics=("parallel",), vmem_limit_bytes=vmem),
    )(q, k_new, v_new, k_cache, v_cache, *lams, gain_col)


def _outproj_kernel(x_ref, a_ref, c_ref, w_ref, o_ref):
    mixed = jnp.concatenate([a_ref[...], c_ref[...]], axis=1)
    o_ref[...] = x_ref[...] + jnp.dot(mixed, w_ref[...], preferred_element_type=F32)


def _outproj(x, attn_out, conv_out, w_out):
    t, d = x.shape
    wa, wc = attn_out.shape[1], conv_out.shape[1]
    tm = min(OUTPROJ_TOKEN_TILE, t)
    assert t % tm == 0 and wa + wc == w_out.shape[0]
    tok = lambda i: (i, 0)
    vmem = _vmem_limit(
        pipelined=[((tm, d), F32), ((tm, wa), BF16), ((tm, wc), BF16), (w_out.shape, BF16),
                   ((tm, d), F32)],
        resident=[((tm, wa + wc), BF16), ((tm, d), F32)])
    return pl.pallas_call(
        _outproj_kernel,
        name="outproj",
        grid=(t // tm,),
        in_specs=[
            pl.BlockSpec((tm, d), tok),
            pl.BlockSpec((tm, wa), tok),
            pl.BlockSpec((tm, wc), tok),
            pl.BlockSpec(w_out.shape, lambda i: (0, 0)),
        ],
        out_specs=pl.BlockSpec((tm, d), tok),
        out_shape=jax.ShapeDtypeStruct((t, d), F32),
        compiler_params=pltpu.CompilerParams(
            dimension_semantics=("parallel",), vmem_limit_bytes=vmem),
    )(x, attn_out, conv_out, w_out)


def _rope_tables(pos, head_dim, rows):
    half = head_dim // 2
    inv = ROPE_THETA ** (-jnp.arange(0, head_dim, 2, dtype=F32) / head_dim)
    ang = pos.astype(F32)[:, None] * inv[None, :]
    cos, sin = jnp.cos(ang), jnp.sin(ang)
    zero = jnp.zeros_like(sin)
    reps = V7X_LANES // head_dim
    cos_t = jnp.tile(jnp.concatenate([cos, cos], axis=1), (rows // pos.shape[0], reps))
    sin_lo = jnp.tile(jnp.concatenate([-sin, zero], axis=1), (rows // pos.shape[0], reps))
    sin_hi = jnp.tile(jnp.concatenate([zero, sin], axis=1), (rows // pos.shape[0], reps))
    return cos_t, sin_lo, sin_hi


def kernel(x_prompt, x_sample, cache_k, cache_v, state_conv, ffn1_norm, ffn1_w_gate, ffn1_w_up,
           ffn1_w_down, mix_norm, w_in, lambda_q1, lambda_k1, lambda_q2, lambda_k2, subln_gain,
           conv_w, w_out, ffn2_norm, ffn2_w_gate, ffn2_w_up, ffn2_w_down, final_norm):
    n_bp, seq_p, d = x_prompt.shape
    n_bs, seq_s, _ = x_sample.shape
    depth, _, past_len, n_heads, key_dim = cache_k.shape
    value_dim = cache_v.shape[-1]
    head_dim = key_dim // 2
    conv_ch = conv_w.shape[-1]
    attn_width = n_heads * key_dim
    assert key_dim == V7X_LANES and value_dim == V7X_LANES and attn_width == conv_ch
    q_scale = head_dim ** -0.5

    tabs_p = _rope_tables(jnp.arange(seq_p), head_dim, seq_p)
    tabs_s = _rope_tables(past_len + jnp.arange(seq_s), head_dim, max(seq_s, INPROJ_TOKEN_TILE))

    xp = x_prompt.reshape(n_bp * seq_p, d)
    xs = x_sample.reshape(n_bs * seq_s, d)
    row = lambda a: a.reshape(1, -1)
    outs = {name: [] for name in ("kp", "vp", "cp", "ks", "vs", "cs")}
    for l in range(depth):
        lambda_init = 0.8 - 0.6 * math.exp(-0.3 * l)
        last = l == depth - 1
        w1g, w1u, w1d = (_to_bf16(w[l]) for w in (ffn1_w_gate, ffn1_w_up, ffn1_w_down))
        w2g, w2u, w2d = (_to_bf16(w[l]) for w in (ffn2_w_gate, ffn2_w_up, ffn2_w_down))
        w_in_l, w_out_l = _to_bf16(w_in[l]), _to_bf16(w_out[l])
        lams = tuple(row(a[l]) for a in (lambda_q1, lambda_k1, lambda_q2, lambda_k2))
        gain = subln_gain[l].reshape(-1, 1)
        fin = row(final_norm)
        zeros_state = jnp.zeros((n_bp, conv_w.shape[1] - 1, conv_ch), F32)

        def layer(x, tabs, state, seq_len, attend):
            x = _ffn(x, row(ffn1_norm[l]), w1g, w1u, w1d, fin, False)
            q, k, v, conv_out, new_state = _inproj(
                x, row(mix_norm[l]), w_in_l, tabs, conv_w[l], state, seq_len, n_heads, q_scale)
            attn_out = attend(q, k, v)
            x = _outproj(x, attn_out, conv_out, w_out_l)
            x = _ffn(x, row(ffn2_norm[l]), w2g, w2u, w2d, fin, last)
            return x, k, v, new_state

        xp, kp, vp, cp = layer(
            xp, tabs_p, zeros_state, seq_p,
            lambda q, k, v: _attn_prompt(q, k, v, lams, gain, n_bp, seq_p, n_heads, lambda_init))
        kc = cache_k[l].reshape(n_bs * past_len * n_heads, key_dim)
        vc = cache_v[l].reshape(n_bs * past_len * n_heads, value_dim)
        xs, ks, vs, cs = layer(
            xs, tabs_s, state_conv[l], seq_s,
            lambda q, k, v: _attn_sample(q, k, v, kc, vc, lams, gain, n_bs, seq_s, n_heads,
                                         lambda_init))
        outs["kp"].append(kp.reshape(n_bp, seq_p, n_heads, key_dim))
        outs["vp"].append(vp.reshape(n_bp, seq_p, n_heads, value_dim))
        outs["cp"].append(cp)
        outs["ks"].append(ks.reshape(n_bs, seq_s, n_heads, key_dim))
        outs["vs"].append(vs.reshape(n_bs, seq_s, n_heads, value_dim))
        outs["cs"].append(cs)
    return (xp.reshape(x_prompt.shape), xs.reshape(x_sample.shape),
            jnp.stack(outs["kp"]), jnp.stack(outs["vp"]), jnp.stack(outs["cp"]),
            jnp.stack(outs["ks"]), jnp.stack(outs["vs"]), jnp.stack(outs["cs"]))
```

```python
import functools
import math

import jax
import jax.numpy as jnp
from jax import lax
from jax.experimental import pallas as pl
from jax.experimental.pallas import tpu as pltpu

F32 = jnp.float32
BF16 = jnp.bfloat16

CHUNK = 64
ROPE_THETA = 10000.0
EPS = 1e-6
MASK_VALUE = -1e30

V7X_LANES = 128
V7X_VMEM_BYTES = 64 * 1024 * 1024
VMEM_BUDGET_BYTES = V7X_VMEM_BYTES * 7 // 8

CAST_BLOCK_BYTES = 6 * 1024 * 1024
FFN_TOKEN_TILE = 512
FFN_COL_TILE = 512
INPROJ_TOKEN_TILE = 256
ATTN_BLOCK = 512
ONES_ROWS = 16
assert CHUNK & (CHUNK - 1) == 0


def _nbytes(shape, dtype):
    return math.prod(shape) * jnp.dtype(dtype).itemsize


def _vmem_limit(pipelined, resident):
    need = 2 * sum(_nbytes(s, d) for s, d in pipelined) + sum(_nbytes(s, d) for s, d in resident)
    return int(min(VMEM_BUDGET_BYTES, max(need, 16 * 1024 * 1024)))


def _rms(x, gain):
    return (x * lax.rsqrt(jnp.mean(x * x, axis=-1, keepdims=True) + EPS)) * gain


def _cast_kernel(x_ref, o_ref):
    o_ref[...] = x_ref[...].astype(o_ref.dtype)


def _to_bf16(w):
    r, c = w.shape
    slab = r
    while slab % 32 == 0 and _nbytes((slab, c), w.dtype) > CAST_BLOCK_BYTES:
        slab //= 2
    return pl.pallas_call(
        _cast_kernel,
        name="cast",
        grid=(r // slab,),
        in_specs=[pl.BlockSpec((slab, c), lambda i: (i, 0))],
        out_specs=pl.BlockSpec((slab, c), lambda i: (i, 0)),
        out_shape=jax.ShapeDtypeStruct(w.shape, BF16),
        compiler_params=pltpu.CompilerParams(
            dimension_semantics=("parallel",), vmem_limit_bytes=VMEM_BUDGET_BYTES),
    )(w)


def _ffn_kernel(*refs, mix, apply_final_norm):
    if mix:
        (x_ref, a_ref, c_ref, wo_ref, nw_ref, wg_ref, wu_ref, wd_ref, fn_ref, o_ref,
         h_ref, res_ref) = refs
    else:
        x_ref, nw_ref, wg_ref, wu_ref, wd_ref, fn_ref, o_ref, h_ref = refs
        res_ref = x_ref
    f = pl.program_id(1)

    @pl.when(f == 0)
    def _():
        if mix:
            mixed = jnp.concatenate([a_ref[...], c_ref[...]], axis=1)
            res_ref[...] = x_ref[...] + jnp.dot(mixed, wo_ref[...], preferred_element_type=F32)
        h_ref[...] = _rms(res_ref[...], nw_ref[...]).astype(BF16)
        o_ref[...] = jnp.zeros_like(o_ref)

    h = h_ref[...]
    gate = jnp.dot(h, wg_ref[...], preferred_element_type=F32)
    up = jnp.dot(h, wu_ref[...], preferred_element_type=F32)
    act = ((gate * jax.nn.sigmoid(gate)) * up).astype(BF16)
    o_ref[...] += jnp.dot(act, wd_ref[...], preferred_element_type=F32)

    @pl.when(f == pl.num_programs(1) - 1)
    def _():
        y = res_ref[...] + 0.5 * o_ref[...]
        if apply_final_norm:
            y = _rms(y, fn_ref[...])
        o_ref[...] = y


def _ffn(x, norm_w, w_gate, w_up, w_down, final_w, apply_final_norm, mix=None):
    t, d = x.shape
    ffn = w_gate.shape[1]
    tm, tf = min(FFN_TOKEN_TILE, t), FFN_COL_TILE
    assert t % tm == 0 and ffn % tf == 0
    tok = lambda i, f: (i, 0)
    const = lambda i, f: (0, 0)
    pipelined = [((tm, d), F32), ((tm, d), F32), ((d, tf), BF16), ((d, tf), BF16), ((tf, d), BF16)]
    resident = [((tm, d), BF16), ((tm, tf), F32), ((tm, tf), F32), ((tm, tf), F32), ((tm, d), F32)]
    x_specs, x_args, scratch = [pl.BlockSpec((tm, d), tok)], [x], [pltpu.VMEM((tm, d), BF16)]
    if mix is not None:
        attn_out, conv_out, w_out = mix
        wa, wc = attn_out.shape[1], conv_out.shape[1]
        assert wa + wc == w_out.shape[0] and w_out.shape[1] == d
        x_specs += [pl.BlockSpec((tm, wa), tok), pl.BlockSpec((tm, wc), tok),
                    pl.BlockSpec(w_out.shape, const, pipeline_mode=pl.Buffered(1))]
        x_args += [attn_out, conv_out, w_out]
        scratch.append(pltpu.VMEM((tm, d), F32))
        pipelined += [((tm, wa), BF16), ((tm, wc), BF16)]
        resident += [(w_out.shape, BF16), ((tm, d), F32), ((tm, d), F32)]
    return pl.pallas_call(
        functools.partial(_ffn_kernel, mix=mix is not None, apply_final_norm=apply_final_norm),
        name="ffn",
        grid=(t // tm, ffn // tf),
        in_specs=x_specs + [
            pl.BlockSpec((1, d), const),
            pl.BlockSpec((d, tf), lambda i, f: (0, f)),
            pl.BlockSpec((d, tf), lambda i, f: (0, f)),
            pl.BlockSpec((tf, d), lambda i, f: (f, 0)),
            pl.BlockSpec((1, d), const),
        ],
        out_specs=pl.BlockSpec((tm, d), tok),
        out_shape=jax.ShapeDtypeStruct((t, d), F32),
        scratch_shapes=scratch,
        compiler_params=pltpu.CompilerParams(
            dimension_semantics=("parallel", "arbitrary"),
            vmem_limit_bytes=_vmem_limit(pipelined, resident)),
    )(*x_args, norm_w, w_gate, w_up, w_down, final_w)


def _inproj_kernel(x_ref, nw_ref, w_ref, cos_ref, sin_lo_ref, sin_hi_ref, cw_ref, st_ref,
                   q_ref, k_ref, v_ref, co_ref, sto_ref, carry_ref,
                   *, width, n_heads, n_seq, tiles_per_seq, q_scale):
    i = pl.program_id(0)
    tm = x_ref.shape[0]
    seq_rows = tm // n_seq
    h = _rms(x_ref[...], nw_ref[...]).astype(BF16)

    def proj(col):
        return jnp.dot(h, w_ref[:, col * width:(col + 1) * width], preferred_element_type=F32)

    cos, sin_lo, sin_hi = cos_ref[...], sin_lo_ref[...], sin_hi_ref[...]

    def rope_store(x, out_ref, scale):
        for hd in range(n_heads):
            xh = x[:, hd * V7X_LANES:(hd + 1) * V7X_LANES]
            r = (xh * cos + pltpu.roll(xh, V7X_LANES - 32, axis=1) * sin_lo
                 + pltpu.roll(xh, 32, axis=1) * sin_hi)
            if scale is not None:
                r = r * scale
            out_ref[:, hd * V7X_LANES:(hd + 1) * V7X_LANES] = r.astype(out_ref.dtype)

    rope_store(proj(0), q_ref, q_scale)
    rope_store(proj(1), k_ref, None)
    v_ref[...] = proj(2)

    cb = proj(3)
    u = proj(4) * proj(5)
    if tiles_per_seq > 1:
        @pl.when(i % tiles_per_seq == 0)
        def _():
            carry_ref[0:2, :] = st_ref[0]
    for s in range(n_seq):
        rows = slice(s * seq_rows, (s + 1) * seq_rows)
        us = u[rows]
        if tiles_per_seq > 1:
            hist0, hist1 = carry_ref[0:1, :], carry_ref[1:2, :]
        else:
            hist0, hist1 = st_ref[s, 0:1, :], st_ref[s, 1:2, :]
        row = lax.broadcasted_iota(jnp.int32, us.shape, 0)
        back1 = pltpu.roll(us, 1, axis=0)
        back2 = pltpu.roll(us, 2, axis=0)
        u1 = jnp.where(row == 0, hist1, back1)
        u2 = jnp.where(row == 0, hist0, jnp.where(row == 1, hist1, back2))
        conv = cw_ref[0:1, :] * u2 + cw_ref[1:2, :] * u1 + cw_ref[2:3, :] * us
        co_ref[rows, :] = (cb[rows] * conv).astype(co_ref.dtype)
        last = back2[0:2, :]
        sto_ref[s] = last
        if tiles_per_seq > 1:
            carry_ref[0:2, :] = last


def _inproj(x, norm_w, w_in, rope_tabs, conv_w, conv_state, seq_len, n_heads, q_scale):
    t, d = x.shape
    width = w_in.shape[1] // 6
    n_batch = conv_state.shape[0]
    hist = conv_state.shape[1]
    assert hist == 2 and seq_len >= hist and t == n_batch * seq_len
    tm = INPROJ_TOKEN_TILE
    if seq_len >= tm:
        assert seq_len % tm == 0
        n_seq, tiles_per_seq = 1, seq_len // tm
    else:
        assert tm % seq_len == 0 and seq_len % 8 == 0
        n_seq, tiles_per_seq = tm // seq_len, 1
    cos, sin_lo, sin_hi = rope_tabs
    pos_tiles = cos.shape[0] // tm
    tab_spec = pl.BlockSpec((tm, V7X_LANES), lambda i: (i % pos_tiles, 0))
    tok = lambda i: (i, 0)
    const = lambda i: (0, 0)
    st_spec = pl.BlockSpec((n_seq, hist, width), lambda i: (i // tiles_per_seq, 0, 0))
    vmem = _vmem_limit(
        pipelined=[((tm, d), F32), ((tm, width), BF16), ((tm, width), F32), ((tm, width), F32),
                   ((tm, width), BF16), ((tm, V7X_LANES), F32), ((tm, V7X_LANES), F32),
                   ((tm, V7X_LANES), F32)],
        resident=[(w_in.shape, BF16), ((tm, d), BF16)] + [((tm, width), F32)] * 8)
    return pl.pallas_call(
        functools.partial(_inproj_kernel, width=width, n_heads=n_heads, n_seq=n_seq,
                          tiles_per_seq=tiles_per_seq, q_scale=q_scale),
        name="inproj",
        grid=(t // tm,),
        in_specs=[
            pl.BlockSpec((tm, d), tok),
            pl.BlockSpec((1, d), const),
            pl.BlockSpec(w_in.shape, const, pipeline_mode=pl.Buffered(1)),
            tab_spec, tab_spec, tab_spec,
            pl.BlockSpec(conv_w.shape, const),
            st_spec,
        ],
        out_specs=[
            pl.BlockSpec((tm, width), tok),
            pl.BlockSpec((tm, width), tok),
            pl.BlockSpec((tm, width), tok),
            pl.BlockSpec((tm, width), tok),
            st_spec,
        ],
        out_shape=[
            jax.ShapeDtypeStruct((t, width), BF16),
            jax.ShapeDtypeStruct((t, width), F32),
            jax.ShapeDtypeStruct((t, width), F32),
            jax.ShapeDtypeStruct((t, width), BF16),
            jax.ShapeDtypeStruct(conv_state.shape, F32),
        ],
        scratch_shapes=[pltpu.VMEM((8, width), F32)],
        compiler_params=pltpu.CompilerParams(
            dimension_semantics=("arbitrary",), vmem_limit_bytes=vmem),
    )(x, norm_w, w_in, cos, sin_lo, sin_hi, conv_w, conv_state)


def _stack_components(q):
    lane = lax.broadcasted_iota(jnp.int32, q.shape, 1)
    half = q.shape[1] // 2
    zero = jnp.zeros_like(q)
    return jnp.concatenate([jnp.where(lane < half, q, zero), jnp.where(lane >= half, q, zero)], axis=0)


def _visible(shape, q_start, k_start):
    assert CHUNK & (CHUNK - 1) == 0
    rows = shape[1] // 2
    c = lax.broadcasted_iota(jnp.int32, shape, 1)
    q_pos = q_start + jnp.where(c >= rows, c - rows, c)
    k_pos = k_start + lax.broadcasted_iota(jnp.int32, shape, 0)
    return k_pos < (q_pos | (CHUNK - 1)) + 1


def _scores(k, q2):
    return lax.dot_general(k, q2, (((1,), (1,)), ((), ())), preferred_element_type=F32)


def _weighted_values(v, p):
    return lax.dot_general(v, p, (((0,), (0,)), ((), ())), preferred_element_type=F32)


def _lambda(lq1_ref, lk1_ref, lq2_ref, lk2_ref, lambda_init):
    s1 = jnp.sum(lq1_ref[...] * lk1_ref[...], axis=1, keepdims=True)
    s2 = jnp.sum(lq2_ref[...] * lk2_ref[...], axis=1, keepdims=True)
    return jnp.exp(s1) - jnp.exp(s2) + lambda_init


def _diff_finish(acc, l, lam, gain_col, lambda_init):
    rows = acc.shape[1] // 2
    o = acc[:, :rows] / l[:, :rows] - lam * (acc[:, rows:] / l[:, rows:])
    y = (o * lax.rsqrt(jnp.mean(o * o, axis=0, keepdims=True) + EPS)) * gain_col
    return (y * (1.0 - lambda_init)).T


def _attn_prompt_kernel(q_ref, qn_ref, k_ref, v_ref, lq1_ref, lk1_ref, lq2_ref, lk2_ref, g_ref, o_ref,
                        kb_ref, vt_ref, q2_ref, s_ref, bias_ref, m_ref, acc_ref, *, lambda_init):
    i = pl.program_id(2)
    blk = q_ref.shape[0]
    e = v_ref.shape[1]
    n_blocks = kb_ref.shape[0]
    diag_slot = 2

    def scores(kb, slot):
        s_ref[slot] = _scores(kb_ref[kb], q2_ref[...])

    def accumulate(kb, slot, diagonal):
        s = s_ref[slot]
        if diagonal:
            s = s + bias_ref[...]
        m_prev = m_ref[...]
        m_new = jnp.maximum(m_prev, jnp.max(s, axis=0, keepdims=True))
        alpha = jnp.exp(m_prev - m_new)
        p = jnp.exp(s - m_new).astype(BF16)
        acc_ref[...] = alpha * acc_ref[...] + jnp.dot(vt_ref[kb], p, preferred_element_type=F32)
        m_ref[...] = m_new

    @pl.when(i == 0)
    def _():
        bias_ref[...] = jnp.where(_visible(bias_ref.shape, 0, 0), 0.0, MASK_VALUE)
        ones = jnp.ones((ONES_ROWS, blk), BF16)
        for j in range(n_blocks):
            rows = slice(j * blk, (j + 1) * blk)
            kb_ref[j] = k_ref[rows, :].astype(BF16)
            vt_ref[j] = jnp.concatenate([v_ref[rows, :].T.astype(BF16), ones], axis=0)
        q2_ref[...] = _stack_components(q_ref[...])
        scores(0, diag_slot)

    m_ref[...] = jnp.full_like(m_ref, MASK_VALUE)
    acc_ref[...] = jnp.zeros_like(acc_ref)

    def pair(j, carry):
        scores(2 * j + 1, 1)
        accumulate(2 * j, 0, False)
        scores(2 * j + 2, 0)
        accumulate(2 * j + 1, 1, False)
        return carry

    lax.fori_loop(0, jnp.maximum(i - 1, 0) // 2, pair, 0)

    @pl.when(i % 2 == 1)
    def _():
        scores(i, diag_slot)
        accumulate(i - 1, 0, False)

    @pl.when((i >= 2) & (i % 2 == 0))
    def _():
        scores(i - 1, 1)
        accumulate(i - 2, 0, False)
        scores(i, diag_slot)
        accumulate(i - 1, 1, False)

    q2_ref[...] = _stack_components(qn_ref[...])
    scores(0, 0)
    accumulate(i, diag_slot, True)

    lam = _lambda(lq1_ref, lk1_ref, lq2_ref, lk2_ref, lambda_init)
    o_ref[...] = _diff_finish(acc_ref[0:e, :], acc_ref[e:e + 1, :], lam, g_ref[...],
                              lambda_init).astype(o_ref.dtype)


def _attn_prompt(q, k, v, lams, gain_col, n_batch, seq_len, n_heads, lambda_init):
    t, width = q.shape
    e = width // n_heads
    blk = ATTN_BLOCK
    assert seq_len % blk == 0 and blk % CHUNK == 0 and e == V7X_LANES
    nq = seq_len // blk
    lam_spec = pl.BlockSpec(lams[0].shape, lambda b, h, i: (0, 0))
    n_blocks = seq_len // blk
    scratch = [((n_blocks, blk, e), BF16), ((n_blocks, e + ONES_ROWS, blk), BF16), ((2 * blk, e), BF16),
               ((3, blk, 2 * blk), F32), ((blk, 2 * blk), F32), ((1, 2 * blk), F32),
               ((e + ONES_ROWS, 2 * blk), F32)]
    vmem = _vmem_limit(
        pipelined=[((blk, e), BF16), ((seq_len, e), F32), ((seq_len, e), F32), ((blk, e), BF16)],
        resident=scratch + [((blk, 2 * blk), F32)] * 6)
    return pl.pallas_call(
        functools.partial(_attn_prompt_kernel, lambda_init=lambda_init),
        name="attn_prompt",
        grid=(n_batch, n_heads, nq),
        in_specs=[
            pl.BlockSpec((blk, e), lambda b, h, i: (b * nq + i, h)),
            pl.BlockSpec((blk, e), lambda b, h, i: (b * nq + jnp.minimum(i + 1, nq - 1), h)),
            pl.BlockSpec((seq_len, e), lambda b, h, i: (b, h)),
            pl.BlockSpec((seq_len, e), lambda b, h, i: (b, h)),
            lam_spec, lam_spec, lam_spec, lam_spec,
            pl.BlockSpec(gain_col.shape, lambda b, h, i: (0, 0)),
        ],
        out_specs=pl.BlockSpec((blk, e), lambda b, h, i: (b * nq + i, h)),
        out_shape=jax.ShapeDtypeStruct((t, width), BF16),
        scratch_shapes=[pltpu.VMEM(s, d) for s, d in scratch],
        compiler_params=pltpu.CompilerParams(
            dimension_semantics=("parallel", "parallel", "arbitrary"), vmem_limit_bytes=vmem),
    )(q, q, k, v, *lams, gain_col)


def _attn_sample_kernel(q_ref, kn_ref, vn_ref, kc_ref, vc_ref, lq1_ref, lk1_ref, lq2_ref, lk2_ref,
                        g_ref, o_ref, *, n_heads, past_len, lambda_init):
    rows = q_ref.shape[0]
    e = q_ref.shape[1] // n_heads
    lam = _lambda(lq1_ref, lk1_ref, lq2_ref, lk2_ref, lambda_init)
    gain_col = g_ref[...]
    visible = _visible((rows, 2 * rows), past_len, past_len)
    for hd in range(n_heads):
        cols = slice(hd * e, (hd + 1) * e)
        kc = kc_ref[pl.ds(hd, past_len, stride=n_heads), :].astype(BF16)
        vc = vc_ref[pl.ds(hd, past_len, stride=n_heads), :].astype(BF16)
        q2 = _stack_components(q_ref[:, cols])
        s_c = _scores(kc, q2)
        s_n = jnp.where(visible, _scores(kn_ref[:, cols].astype(BF16), q2), MASK_VALUE)
        m = jnp.maximum(jnp.max(s_c, axis=0, keepdims=True), jnp.max(s_n, axis=0, keepdims=True))
        p_c = jnp.exp(s_c - m)
        p_n = jnp.exp(s_n - m)
        l = jnp.sum(p_c, axis=0, keepdims=True) + jnp.sum(p_n, axis=0, keepdims=True)
        acc = (_weighted_values(vc, p_c.astype(BF16))
               + _weighted_values(vn_ref[:, cols].astype(BF16), p_n.astype(BF16)))
        o_ref[:, cols] = _diff_finish(acc, l, lam, gain_col, lambda_init).astype(o_ref.dtype)


def _attn_sample(q, k_new, v_new, k_cache, v_cache, lams, gain_col, n_batch, seq_len, n_heads,
                 lambda_init):
    t, width = q.shape
    e = width // n_heads
    past_len = k_cache.shape[0] // (n_batch * n_heads)
    assert seq_len % 8 == 0 and past_len % 8 == 0 and k_cache.shape[1] == e
    tok = lambda b: (b, 0)
    lam_spec = pl.BlockSpec(lams[0].shape, lambda b: (0, 0))
    vmem = _vmem_limit(
        pipelined=[((past_len * n_heads, e), F32)] * 2 + [((seq_len, width), F32)] * 4,
        resident=[((past_len, 2 * seq_len), F32)] * 4 + [((past_len, e), F32)] * 4)
    return pl.pallas_call(
        functools.partial(_attn_sample_kernel, n_heads=n_heads, past_len=past_len,
                          lambda_init=lambda_init),
        name="attn_sample",
        grid=(n_batch,),
        in_specs=[
            pl.BlockSpec((seq_len, width), tok),
            pl.BlockSpec((seq_len, width), tok),
            pl.BlockSpec((seq_len, width), tok),
            pl.BlockSpec((past_len * n_heads, e), tok),
            pl.BlockSpec((past_len * n_heads, e), tok),
            lam_spec, lam_spec, lam_spec, lam_spec,
            pl.BlockSpec(gain_col.shape, lambda b: (0, 0)),
        ],
        out_specs=pl.BlockSpec((seq_len, width), tok),
        out_shape=jax.ShapeDtypeStruct((t, width), BF16),
        compiler_params=pltpu.CompilerParams(
            dimension_semantics=("parallel",), vmem_limit_bytes=vmem),
    )(q, k_new, v_new, k_cache, v_cache, *lams, gain_col)


def _rope_tables(pos, head_dim, rows):
    half = head_dim // 2
    inv = ROPE_THETA ** (-jnp.arange(0, head_dim, 2, dtype=F32) / head_dim)
    ang = pos.astype(F32)[:, None] * inv[None, :]
    cos, sin = jnp.cos(ang), jnp.sin(ang)
    zero = jnp.zeros_like(sin)
    reps = V7X_LANES // head_dim
    cos_t = jnp.tile(jnp.concatenate([cos, cos], axis=1), (rows // pos.shape[0], reps))
    sin_lo = jnp.tile(jnp.concatenate([-sin, zero], axis=1), (rows // pos.shape[0], reps))
    sin_hi = jnp.tile(jnp.concatenate([zero, sin], axis=1), (rows // pos.shape[0], reps))
    return cos_t, sin_lo, sin_hi


def kernel(x_prompt, x_sample, cache_k, cache_v, state_conv, ffn1_norm, ffn1_w_gate, ffn1_w_up,
           ffn1_w_down, mix_norm, w_in, lambda_q1, lambda_k1, lambda_q2, lambda_k2, subln_gain,
           conv_w, w_out, ffn2_norm, ffn2_w_gate, ffn2_w_up, ffn2_w_down, final_norm):
    n_bp, seq_p, d = x_prompt.shape
    n_bs, seq_s, _ = x_sample.shape
    depth, _, past_len, n_heads, key_dim = cache_k.shape
    value_dim = cache_v.shape[-1]
    head_dim = key_dim // 2
    conv_ch = conv_w.shape[-1]
    attn_width = n_heads * key_dim
    assert key_dim == V7X_LANES and value_dim == V7X_LANES and attn_width == conv_ch
    q_scale = head_dim ** -0.5

    tabs_p = _rope_tables(jnp.arange(seq_p), head_dim, seq_p)
    tabs_s = _rope_tables(past_len + jnp.arange(seq_s), head_dim, max(seq_s, INPROJ_TOKEN_TILE))

    xp = x_prompt.reshape(n_bp * seq_p, d)
    xs = x_sample.reshape(n_bs * seq_s, d)
    row = lambda a: a.reshape(1, -1)
    outs = {name: [] for name in ("kp", "vp", "cp", "ks", "vs", "cs")}
    for l in range(depth):
        lambda_init = 0.8 - 0.6 * math.exp(-0.3 * l)
        last = l == depth - 1
        w1g, w1u, w1d = (_to_bf16(w[l]) for w in (ffn1_w_gate, ffn1_w_up, ffn1_w_down))
        w2g, w2u, w2d = (_to_bf16(w[l]) for w in (ffn2_w_gate, ffn2_w_up, ffn2_w_down))
        w_in_l, w_out_l = _to_bf16(w_in[l]), _to_bf16(w_out[l])
        lams = tuple(row(a[l]) for a in (lambda_q1, lambda_k1, lambda_q2, lambda_k2))
        gain = subln_gain[l].reshape(-1, 1)
        fin = row(final_norm)
        zeros_state = jnp.zeros((n_bp, conv_w.shape[1] - 1, conv_ch), F32)

        def layer(x, tabs, state, seq_len, attend):
            x = _ffn(x, row(ffn1_norm[l]), w1g, w1u, w1d, fin, False)
            q, k, v, conv_out, new_state = _inproj(
                x, row(mix_norm[l]), w_in_l, tabs, conv_w[l], state, seq_len, n_heads, q_scale)
            attn_out = attend(q, k, v)
            x = _ffn(x, row(ffn2_norm[l]), w2g, w2u, w2d, fin, last, mix=(attn_out, conv_out, w_out_l))
            return x, k, v, new_state

        xp, kp, vp, cp = layer(
            xp, tabs_p, zeros_state, seq_p,
            lambda q, k, v: _attn_prompt(q, k, v, lams, gain, n_bp, seq_p, n_heads, lambda_init))
        kc = cache_k[l].reshape(n_bs * past_len * n_heads, key_dim)
        vc = cache_v[l].reshape(n_bs * past_len * n_heads, value_dim)
        xs, ks, vs, cs = layer(
            xs, tabs_s, state_conv[l], seq_s,
            lambda q, k, v: _attn_sample(q, k, v, kc, vc, lams, gain, n_bs, seq_s, n_heads,
                                         lambda_init))
        outs["kp"].append(kp.reshape(n_bp, seq_p, n_heads, key_dim))
        outs["vp"].append(vp.reshape(n_bp, seq_p, n_heads, value_dim))
        outs["cp"].append(cp)
        outs["ks"].append(ks.reshape(n_bs, seq_s, n_heads, key_dim))
        outs["vs"].append(vs.reshape(n_bs, seq_s, n_heads, value_dim))
        outs["cs"].append(cs)
    return (xp.reshape(x_prompt.shape), xs.reshape(x_sample.shape),
            jnp.stack(outs["kp"]), jnp.stack(outs["vp"]), jnp.stack(outs["cp"]),
            jnp.stack(outs["ks"]), jnp.stack(outs["vs"]), jnp.stack(outs["cs"]))
```

```python
import functools
import math

import jax
import jax.numpy as jnp
from jax import lax
from jax.experimental import pallas as pl
from jax.experimental.pallas import tpu as pltpu

F32 = jnp.float32
BF16 = jnp.bfloat16

CHUNK = 64
ROPE_THETA = 10000.0
EPS = 1e-6
MASK_VALUE = -1e30

V7X_LANES = 128
V7X_VMEM_BYTES = 64 * 1024 * 1024
VMEM_BUDGET_BYTES = V7X_VMEM_BYTES * 7 // 8

CAST_BLOCK_BYTES = 6 * 1024 * 1024
FFN_TOKEN_TILE = 512
FFN_COL_TILE = 512
INPROJ_TOKEN_TILE = 256
ATTN_BLOCK = 512
ONES_ROWS = 16
assert CHUNK & (CHUNK - 1) == 0


def _nbytes(shape, dtype):
    return math.prod(shape) * jnp.dtype(dtype).itemsize


def _vmem_limit(pipelined, resident):
    need = 2 * sum(_nbytes(s, d) for s, d in pipelined) + sum(_nbytes(s, d) for s, d in resident)
    return int(min(VMEM_BUDGET_BYTES, max(need, 16 * 1024 * 1024)))


def _rms(x, gain):
    return (x * lax.rsqrt(jnp.mean(x * x, axis=-1, keepdims=True) + EPS)) * gain


def _cast_kernel(x_ref, o_ref):
    o_ref[...] = x_ref[...].astype(o_ref.dtype)


def _to_bf16(w):
    r, c = w.shape
    slab = r
    while slab % 32 == 0 and _nbytes((slab, c), w.dtype) > CAST_BLOCK_BYTES:
        slab //= 2
    return pl.pallas_call(
        _cast_kernel,
        name="cast",
        grid=(r // slab,),
        in_specs=[pl.BlockSpec((slab, c), lambda i: (i, 0))],
        out_specs=pl.BlockSpec((slab, c), lambda i: (i, 0)),
        out_shape=jax.ShapeDtypeStruct(w.shape, BF16),
        compiler_params=pltpu.CompilerParams(
            dimension_semantics=("parallel",), vmem_limit_bytes=VMEM_BUDGET_BYTES),
    )(w)


def _ffn_kernel(*refs, mix, apply_final_norm):
    if mix:
        (x_ref, a_ref, c_ref, wo_ref, nw_ref, wg_ref, wu_ref, wd_ref, fn_ref, o_ref,
         h_ref, res_ref) = refs
    else:
        x_ref, nw_ref, wg_ref, wu_ref, wd_ref, fn_ref, o_ref, h_ref = refs
        res_ref = x_ref
    f = pl.program_id(1)

    @pl.when(f == 0)
    def _():
        if mix:
            mixed = jnp.concatenate([a_ref[...], c_ref[...]], axis=1)
            res_ref[...] = x_ref[...] + jnp.dot(mixed, wo_ref[...], preferred_element_type=F32)
        h_ref[...] = _rms(res_ref[...], nw_ref[...]).astype(BF16)
        o_ref[...] = jnp.zeros_like(o_ref)

    h = h_ref[...]
    gate = jnp.dot(h, wg_ref[...], preferred_element_type=F32)
    up = jnp.dot(h, wu_ref[...], preferred_element_type=F32)
    act = ((gate * jax.nn.sigmoid(gate)) * up).astype(BF16)
    o_ref[...] += jnp.dot(act, wd_ref[...], preferred_element_type=F32)

    @pl.when(f == pl.num_programs(1) - 1)
    def _():
        y = res_ref[...] + 0.5 * o_ref[...]
        if apply_final_norm:
            y = _rms(y, fn_ref[...])
        o_ref[...] = y


def _ffn(x, norm_w, w_gate, w_up, w_down, final_w, apply_final_norm, mix=None):
    t, d = x.shape
    ffn = w_gate.shape[1]
    tm, tf = min(FFN_TOKEN_TILE, t), FFN_COL_TILE
    if mix is None and t == 2 * FFN_TOKEN_TILE:
        tm, tf = t, FFN_COL_TILE // 2
    assert t % tm == 0 and ffn % tf == 0
    tok = lambda i, f: (i, 0)
    const = lambda i, f: (0, 0)
    pipelined = [((tm, d), F32), ((tm, d), F32), ((d, tf), BF16), ((d, tf), BF16), ((tf, d), BF16)]
    resident = [((tm, d), BF16), ((tm, tf), F32), ((tm, tf), F32), ((tm, tf), F32), ((tm, d), F32)]
    x_specs, x_args, scratch = [pl.BlockSpec((tm, d), tok)], [x], [pltpu.VMEM((tm, d), BF16)]
    if mix is not None:
        attn_out, conv_out, w_out = mix
        wa, wc = attn_out.shape[1], conv_out.shape[1]
        assert wa + wc == w_out.shape[0] and w_out.shape[1] == d
        x_specs += [pl.BlockSpec((tm, wa), tok), pl.BlockSpec((tm, wc), tok),
                    pl.BlockSpec(w_out.shape, const, pipeline_mode=pl.Buffered(1))]
        x_args += [attn_out, conv_out, w_out]
        scratch.append(pltpu.VMEM((tm, d), F32))
        pipelined += [((tm, wa), BF16), ((tm, wc), BF16)]
        resident += [(w_out.shape, BF16), ((tm, d), F32), ((tm, d), F32)]
    return pl.pallas_call(
        functools.partial(_ffn_kernel, mix=mix is not None, apply_final_norm=apply_final_norm),
        name="ffn",
        grid=(t // tm, ffn // tf),
        in_specs=x_specs + [
            pl.BlockSpec((1, d), const),
            pl.BlockSpec((d, tf), lambda i, f: (0, f)),
            pl.BlockSpec((d, tf), lambda i, f: (0, f)),
            pl.BlockSpec((tf, d), lambda i, f: (f, 0)),
            pl.BlockSpec((1, d), const),
        ],
        out_specs=pl.BlockSpec((tm, d), tok),
        out_shape=jax.ShapeDtypeStruct((t, d), F32),
        scratch_shapes=scratch,
        compiler_params=pltpu.CompilerParams(
            dimension_semantics=("parallel", "arbitrary"),
            vmem_limit_bytes=_vmem_limit(pipelined, resident)),
    )(*x_args, norm_w, w_gate, w_up, w_down, final_w)


def _inproj_kernel(x_ref, nw_ref, w_ref, cos_ref, sin_lo_ref, sin_hi_ref, cw_ref, st_ref,
                   q_ref, k_ref, v_ref, co_ref, sto_ref, carry_ref,
                   *, width, n_heads, n_seq, tiles_per_seq, q_scale):
    i = pl.program_id(0)
    tm = x_ref.shape[0]
    seq_rows = tm // n_seq
    h = _rms(x_ref[...], nw_ref[...]).astype(BF16)

    def proj(col):
        return jnp.dot(h, w_ref[:, col * width:(col + 1) * width], preferred_element_type=F32)

    def rope_store(x, out_ref, scale):
        cos, sin_lo, sin_hi = cos_ref[...], sin_lo_ref[...], sin_hi_ref[...]
        for hd in range(n_heads):
            xh = x[:, hd * V7X_LANES:(hd + 1) * V7X_LANES]
            r = (xh * cos + pltpu.roll(xh, V7X_LANES - 32, axis=1) * sin_lo
                 + pltpu.roll(xh, 32, axis=1) * sin_hi)
            if scale is not None:
                r = r * scale
            out_ref[:, hd * V7X_LANES:(hd + 1) * V7X_LANES] = r.astype(out_ref.dtype)

    rope_store(proj(0), q_ref, q_scale)
    rope_store(proj(1), k_ref, None)
    v_ref[...] = proj(2)

    cb = proj(3)
    u = proj(4) * proj(5)
    if tiles_per_seq > 1:
        @pl.when(i % tiles_per_seq == 0)
        def _():
            carry_ref[0:2, :] = st_ref[0]
    for s in range(n_seq):
        rows = slice(s * seq_rows, (s + 1) * seq_rows)
        us = u[rows]
        if tiles_per_seq > 1:
            hist0, hist1 = carry_ref[0:1, :], carry_ref[1:2, :]
        else:
            hist0, hist1 = st_ref[s, 0:1, :], st_ref[s, 1:2, :]
        row = lax.broadcasted_iota(jnp.int32, us.shape, 0)
        back1 = pltpu.roll(us, 1, axis=0)
        back2 = pltpu.roll(us, 2, axis=0)
        u1 = jnp.where(row == 0, hist1, back1)
        u2 = jnp.where(row == 0, hist0, jnp.where(row == 1, hist1, back2))
        conv = cw_ref[0:1, :] * u2 + cw_ref[1:2, :] * u1 + cw_ref[2:3, :] * us
        co_ref[rows, :] = (cb[rows] * conv).astype(co_ref.dtype)
        last = back2[0:2, :]
        sto_ref[s] = last
        if tiles_per_seq > 1:
            carry_ref[0:2, :] = last


def _inproj(x, norm_w, w_in, rope_tabs, conv_w, conv_state, seq_len, n_heads, q_scale):
    t, d = x.shape
    width = w_in.shape[1] // 6
    n_batch = conv_state.shape[0]
    hist = conv_state.shape[1]
    assert hist == 2 and seq_len >= hist and t == n_batch * seq_len
    tm = INPROJ_TOKEN_TILE
    if seq_len >= tm:
        assert seq_len % tm == 0
        n_seq, tiles_per_seq = 1, seq_len // tm
    else:
        assert tm % seq_len == 0 and seq_len % 8 == 0
        n_seq, tiles_per_seq = tm // seq_len, 1
    cos, sin_lo, sin_hi = rope_tabs
    pos_tiles = cos.shape[0] // tm
    tab_spec = pl.BlockSpec((tm, V7X_LANES), lambda i: (i % pos_tiles, 0))
    tok = lambda i: (i, 0)
    const = lambda i: (0, 0)
    st_spec = pl.BlockSpec((n_seq, hist, width), lambda i: (i // tiles_per_seq, 0, 0))
    vmem = _vmem_limit(
        pipelined=[((tm, d), F32), ((tm, width), BF16), ((tm, width), F32), ((tm, width), F32),
                   ((tm, width), BF16), ((tm, V7X_LANES), F32), ((tm, V7X_LANES), F32),
                   ((tm, V7X_LANES), F32)],
        resident=[(w_in.shape, BF16), ((tm, d), BF16)] + [((tm, width), F32)] * 8)
    return pl.pallas_call(
        functools.partial(_inproj_kernel, width=width, n_heads=n_heads, n_seq=n_seq,
                          tiles_per_seq=tiles_per_seq, q_scale=q_scale),
        name="inproj",
        grid=(t // tm,),
        in_specs=[
            pl.BlockSpec((tm, d), tok),
            pl.BlockSpec((1, d), const),
            pl.BlockSpec(w_in.shape, const, pipeline_mode=pl.Buffered(1)),
            tab_spec, tab_spec, tab_spec,
            pl.BlockSpec(conv_w.shape, const),
            st_spec,
        ],
        out_specs=[
            pl.BlockSpec((tm, width), tok),
            pl.BlockSpec((tm, width), tok),
            pl.BlockSpec((tm, width), tok),
            pl.BlockSpec((tm, width), tok),
            st_spec,
        ],
        out_shape=[
            jax.ShapeDtypeStruct((t, width), BF16),
            jax.ShapeDtypeStruct((t, width), F32),
            jax.ShapeDtypeStruct((t, width), F32),
            jax.ShapeDtypeStruct((t, width), BF16),
            jax.ShapeDtypeStruct(conv_state.shape, F32),
        ],
        scratch_shapes=[pltpu.VMEM((8, width), F32)],
        compiler_params=pltpu.CompilerParams(
            dimension_semantics=("arbitrary",), vmem_limit_bytes=vmem),
    )(x, norm_w, w_in, cos, sin_lo, sin_hi, conv_w, conv_state)


def _stack_components(q):
    lane = lax.broadcasted_iota(jnp.int32, q.shape, 1)
    half = q.shape[1] // 2
    zero = jnp.zeros_like(q)
    return jnp.concatenate([jnp.where(lane < half, q, zero), jnp.where(lane >= half, q, zero)], axis=0)


def _visible(shape, q_start, k_start):
    assert CHUNK & (CHUNK - 1) == 0
    rows = shape[1] // 2
    c = lax.broadcasted_iota(jnp.int32, shape, 1)
    q_pos = q_start + jnp.where(c >= rows, c - rows, c)
    k_pos = k_start + lax.broadcasted_iota(jnp.int32, shape, 0)
    return k_pos < (q_pos | (CHUNK - 1)) + 1


def _scores(k, q2):
    return lax.dot_general(k, q2, (((1,), (1,)), ((), ())), preferred_element_type=F32)


def _weighted_values(v, p):
    return lax.dot_general(v, p, (((0,), (0,)), ((), ())), preferred_element_type=F32)


def _lambda(lq1_ref, lk1_ref, lq2_ref, lk2_ref, lambda_init):
    s1 = jnp.sum(lq1_ref[...] * lk1_ref[...], axis=1, keepdims=True)
    s2 = jnp.sum(lq2_ref[...] * lk2_ref[...], axis=1, keepdims=True)
    return jnp.exp(s1) - jnp.exp(s2) + lambda_init


def _diff_finish(acc, l, lam, gain_col, lambda_init):
    rows = acc.shape[1] // 2
    o = acc[:, :rows] / l[:, :rows] - lam * (acc[:, rows:] / l[:, rows:])
    y = (o * lax.rsqrt(jnp.mean(o * o, axis=0, keepdims=True) + EPS)) * gain_col
    return (y * (1.0 - lambda_init)).T


def _attn_prompt_kernel(q_ref, qn_ref, k_ref, v_ref, lq1_ref, lk1_ref, lq2_ref, lk2_ref, g_ref, o_ref,
                        kb_ref, vt_ref, q2_ref, s_ref, bias_ref, m_ref, acc_ref, *, lambda_init):
    i = pl.program_id(2)
    blk = q_ref.shape[0]
    e = v_ref.shape[1]
    n_blocks = kb_ref.shape[0]
    diag_slot = 2

    def scores(kb, slot):
        s_ref[slot] = _scores(kb_ref[kb], q2_ref[...])

    def accumulate(kb, slot, diagonal):
        s = s_ref[slot]
        if diagonal:
            s = s + bias_ref[...]
        m_prev = m_ref[...]
        m_new = jnp.maximum(m_prev, jnp.max(s, axis=0, keepdims=True))
        alpha = jnp.exp(m_prev - m_new)
        p = jnp.exp(s - m_new).astype(BF16)
        acc_ref[...] = alpha * acc_ref[...] + jnp.dot(vt_ref[kb], p, preferred_element_type=F32)
        m_ref[...] = m_new

    @pl.when(i == 0)
    def _():
        bias_ref[...] = jnp.where(_visible(bias_ref.shape, 0, 0), 0.0, MASK_VALUE)
        ones = jnp.ones((ONES_ROWS, blk), BF16)
        for j in range(n_blocks):
            rows = slice(j * blk, (j + 1) * blk)
            kb_ref[j] = k_ref[rows, :].astype(BF16)
            vt_ref[j] = jnp.concatenate([v_ref[rows, :].T.astype(BF16), ones], axis=0)
        q2_ref[...] = _stack_components(q_ref[...])
        scores(0, diag_slot)

    m_ref[...] = jnp.full_like(m_ref, MASK_VALUE)
    acc_ref[...] = jnp.zeros_like(acc_ref)

    def pair(j, carry):
        scores(2 * j + 1, 1)
        accumulate(2 * j, 0, False)
        scores(2 * j + 2, 0)
        accumulate(2 * j + 1, 1, False)
        return carry

    lax.fori_loop(0, jnp.maximum(i - 1, 0) // 2, pair, 0)

    @pl.when(i % 2 == 1)
    def _():
        scores(i, diag_slot)
        accumulate(i - 1, 0, False)

    @pl.when((i >= 2) & (i % 2 == 0))
    def _():
        scores(i - 1, 1)
        accumulate(i - 2, 0, False)
        scores(i, diag_slot)
        accumulate(i - 1, 1, False)

    q2_ref[...] = _stack_components(qn_ref[...])
    scores(0, 0)
    accumulate(i, diag_slot, True)

    lam = _lambda(lq1_ref, lk1_ref, lq2_ref, lk2_ref, lambda_init)
    o_ref[...] = _diff_finish(acc_ref[0:e, :], acc_ref[e:e + 1, :], lam, g_ref[...],
                              lambda_init).astype(o_ref.dtype)


def _attn_prompt(q, k, v, lams, gain_col, n_batch, seq_len, n_heads, lambda_init):
    t, width = q.shape
    e = width // n_heads
    blk = ATTN_BLOCK
    assert seq_len % blk == 0 and blk % CHUNK == 0 and e == V7X_LANES
    nq = seq_len // blk
    lam_spec = pl.BlockSpec(lams[0].shape, lambda b, h, i: (0, 0))
    n_blocks = seq_len // blk
    scratch = [((n_blocks, blk, e), BF16), ((n_blocks, e + ONES_ROWS, blk), BF16), ((2 * blk, e), BF16),
               ((3, blk, 2 * blk), F32), ((blk, 2 * blk), F32), ((1, 2 * blk), F32),
               ((e + ONES_ROWS, 2 * blk), F32)]
    vmem = _vmem_limit(
        pipelined=[((blk, e), BF16)] * 3 + [((seq_len, e), F32)] * 2,
        resident=scratch + [((blk, 2 * blk), F32)] * 6)
    return pl.pallas_call(
        functools.partial(_attn_prompt_kernel, lambda_init=lambda_init),
        name="attn_prompt",
        grid=(n_batch, n_heads, nq),
        in_specs=[
            pl.BlockSpec((blk, e), lambda b, h, i: (b * nq + i, h)),
            pl.BlockSpec((blk, e), lambda b, h, i: (b * nq + jnp.minimum(i + 1, nq - 1), h)),
            pl.BlockSpec((seq_len, e), lambda b, h, i: (b, h)),
            pl.BlockSpec((seq_len, e), lambda b, h, i: (b, h)),
            lam_spec, lam_spec, lam_spec, lam_spec,
            pl.BlockSpec(gain_col.shape, lambda b, h, i: (0, 0)),
        ],
        out_specs=pl.BlockSpec((blk, e), lambda b, h, i: (b * nq + i, h)),
        out_shape=jax.ShapeDtypeStruct((t, width), BF16),
        scratch_shapes=[pltpu.VMEM(s, d) for s, d in scratch],
        compiler_params=pltpu.CompilerParams(
            dimension_semantics=("parallel", "parallel", "arbitrary"), vmem_limit_bytes=vmem),
    )(q, q, k, v, *lams, gain_col)


def _attn_sample_kernel(q_ref, kn_ref, vn_ref, kc_ref, vc_ref, lq1_ref, lk1_ref, lq2_ref, lk2_ref,
                        g_ref, o_ref, *, n_heads, past_len, lambda_init):
    rows = q_ref.shape[0]
    e = q_ref.shape[1] // n_heads
    lam = _lambda(lq1_ref, lk1_ref, lq2_ref, lk2_ref, lambda_init)
    gain_col = g_ref[...]
    visible = _visible((rows, 2 * rows), past_len, past_len)
    for hd in range(n_heads):
        cols = slice(hd * e, (hd + 1) * e)
        kc = kc_ref[pl.ds(hd, past_len, stride=n_heads), :].astype(BF16)
        vc = vc_ref[pl.ds(hd, past_len, stride=n_heads), :].astype(BF16)
        q2 = _stack_components(q_ref[:, cols])
        s_c = _scores(kc, q2)
        s_n = jnp.where(visible, _scores(kn_ref[:, cols].astype(BF16), q2), MASK_VALUE)
        m = jnp.maximum(jnp.max(s_c, axis=0, keepdims=True), jnp.max(s_n, axis=0, keepdims=True))
        p_c = jnp.exp(s_c - m)
        p_n = jnp.exp(s_n - m)
        l = jnp.sum(p_c, axis=0, keepdims=True) + jnp.sum(p_n, axis=0, keepdims=True)
        acc = (_weighted_values(vc, p_c.astype(BF16))
               + _weighted_values(vn_ref[:, cols].astype(BF16), p_n.astype(BF16)))
        o_ref[:, cols] = _diff_finish(acc, l, lam, gain_col, lambda_init).astype(o_ref.dtype)


def _attn_sample(q, k_new, v_new, k_cache, v_cache, lams, gain_col, n_batch, seq_len, n_heads,
                 lambda_init):
    t, width = q.shape
    e = width // n_heads
    past_len = k_cache.shape[0] // (n_batch * n_heads)
    assert seq_len % 8 == 0 and past_len % 8 == 0 and k_cache.shape[1] == e
    tok = lambda b: (b, 0)
    lam_spec = pl.BlockSpec(lams[0].shape, lambda b: (0, 0))
    vmem = _vmem_limit(
        pipelined=[((past_len * n_heads, e), F32)] * 2 + [((seq_len, width), F32)] * 4,
        resident=[((past_len, 2 * seq_len), F32)] * 4 + [((past_len, e), F32)] * 4)
    return pl.pallas_call(
        functools.partial(_attn_sample_kernel, n_heads=n_heads, past_len=past_len,
                          lambda_init=lambda_init),
        name="attn_sample",
        grid=(n_batch,),
        in_specs=[
            pl.BlockSpec((seq_len, width), tok),
            pl.BlockSpec((seq_len, width), tok),
            pl.BlockSpec((seq_len, width), tok),
            pl.BlockSpec((past_len * n_heads, e), tok),
            pl.BlockSpec((past_len * n_heads, e), tok),
            lam_spec, lam_spec, lam_spec, lam_spec,
            pl.BlockSpec(gain_col.shape, lambda b: (0, 0)),
        ],
        out_specs=pl.BlockSpec((seq_len, width), tok),
        out_shape=jax.ShapeDtypeStruct((t, width), BF16),
        compiler_params=pltpu.CompilerParams(
            dimension_semantics=("parallel",), vmem_limit_bytes=vmem),
    )(q, k_new, v_new, k_cache, v_cache, *lams, gain_col)


def _rope_tables(pos, head_dim, rows):
    half = head_dim // 2
    inv = ROPE_THETA ** (-jnp.arange(0, head_dim, 2, dtype=F32) / head_dim)
    ang = pos.astype(F32)[:, None] * inv[None, :]
    cos, sin = jnp.cos(ang), jnp.sin(ang)
    zero = jnp.zeros_like(sin)
    reps = V7X_LANES // head_dim
    cos_t = jnp.tile(jnp.concatenate([cos, cos], axis=1), (rows // pos.shape[0], reps))
    sin_lo = jnp.tile(jnp.concatenate([-sin, zero], axis=1), (rows // pos.shape[0], reps))
    sin_hi = jnp.tile(jnp.concatenate([zero, sin], axis=1), (rows // pos.shape[0], reps))
    return cos_t, sin_lo, sin_hi


def kernel(x_prompt, x_sample, cache_k, cache_v, state_conv, ffn1_norm, ffn1_w_gate, ffn1_w_up,
           ffn1_w_down, mix_norm, w_in, lambda_q1, lambda_k1, lambda_q2, lambda_k2, subln_gain,
           conv_w, w_out, ffn2_norm, ffn2_w_gate, ffn2_w_up, ffn2_w_down, final_norm):
    n_bp, seq_p, d = x_prompt.shape
    n_bs, seq_s, _ = x_sample.shape
    depth, _, past_len, n_heads, key_dim = cache_k.shape
    value_dim = cache_v.shape[-1]
    head_dim = key_dim // 2
    conv_ch = conv_w.shape[-1]
    attn_width = n_heads * key_dim
    assert key_dim == V7X_LANES and value_dim == V7X_LANES and attn_width == conv_ch
    q_scale = head_dim ** -0.5

    tabs_p = _rope_tables(jnp.arange(seq_p), head_dim, seq_p)
    tabs_s = _rope_tables(past_len + jnp.arange(seq_s), head_dim, max(seq_s, INPROJ_TOKEN_TILE))

    xp = x_prompt.reshape(n_bp * seq_p, d)
    xs = x_sample.reshape(n_bs * seq_s, d)
    row = lambda a: a.reshape(1, -1)
    outs = {name: [] for name in ("kp", "vp", "cp", "ks", "vs", "cs")}
    for l in range(depth):
        lambda_init = 0.8 - 0.6 * math.exp(-0.3 * l)
        last = l == depth - 1
        w1g, w1u, w1d = (_to_bf16(w[l]) for w in (ffn1_w_gate, ffn1_w_up, ffn1_w_down))
        w2g, w2u, w2d = (_to_bf16(w[l]) for w in (ffn2_w_gate, ffn2_w_up, ffn2_w_down))
        w_in_l, w_out_l = _to_bf16(w_in[l]), _to_bf16(w_out[l])
        lams = tuple(row(a[l]) for a in (lambda_q1, lambda_k1, lambda_q2, lambda_k2))
        gain = subln_gain[l].reshape(-1, 1)
        fin = row(final_norm)
        zeros_state = jnp.zeros((n_bp, conv_w.shape[1] - 1, conv_ch), F32)

        def layer(x, tabs, state, seq_len, attend):
            x = _ffn(x, row(ffn1_norm[l]), w1g, w1u, w1d, fin, False)
            q, k, v, conv_out, new_state = _inproj(
                x, row(mix_norm[l]), w_in_l, tabs, conv_w[l], state, seq_len, n_heads, q_scale)
            attn_out = attend(q, k, v)
            x = _ffn(x, row(ffn2_norm[l]), w2g, w2u, w2d, fin, last, mix=(attn_out, conv_out, w_out_l))
            return x, k, v, new_state

        xp, kp, vp, cp = layer(
            xp, tabs_p, zeros_state, seq_p,
            lambda q, k, v: _attn_prompt(q, k, v, lams, gain, n_bp, seq_p, n_heads, lambda_init))
        kc = cache_k[l].reshape(n_bs * past_len * n_heads, key_dim)
        vc = cache_v[l].reshape(n_bs * past_len * n_heads, value_dim)
        xs, ks, vs, cs = layer(
            xs, tabs_s, state_conv[l], seq_s,
            lambda q, k, v: _attn_sample(q, k, v, kc, vc, lams, gain, n_bs, seq_s, n_heads,
                                         lambda_init))
        outs["kp"].append(kp.reshape(n_bp, seq_p, n_heads, key_dim))
        outs["vp"].append(vp.reshape(n_bp, seq_p, n_heads, value_dim))
        outs["cp"].append(cp)
        outs["ks"].append(ks.reshape(n_bs, seq_s, n_heads, key_dim))
        outs["vs"].append(vs.reshape(n_bs, seq_s, n_heads, value_dim))
        outs["cs"].append(cs)
    return (xp.reshape(x_prompt.shape), xs.reshape(x_sample.shape),
            jnp.stack(outs["kp"]), jnp.stack(outs["vp"]), jnp.stack(outs["cp"]),
            jnp.stack(outs["ks"]), jnp.stack(outs["vs"]), jnp.stack(outs["cs"]))
```

```python
import functools
import math

import jax
import jax.numpy as jnp
from jax import lax
from jax.experimental import pallas as pl
from jax.experimental.pallas import tpu as pltpu

F32 = jnp.float32
BF16 = jnp.bfloat16

CHUNK = 64
ROPE_THETA = 10000.0
EPS = 1e-6
MASK_VALUE = -1e30

V7X_LANES = 128
V7X_VMEM_BYTES = 64 * 1024 * 1024
VMEM_BUDGET_BYTES = V7X_VMEM_BYTES * 7 // 8

CAST_BLOCK_BYTES = 6 * 1024 * 1024
FFN_TOKEN_TILE = 512
FFN_COL_TILE = 512
INPROJ_TOKEN_TILE = 256
ATTN_BLOCK = 512
ONES_ROWS = 16
assert CHUNK & (CHUNK - 1) == 0


def _nbytes(shape, dtype):
    return math.prod(shape) * jnp.dtype(dtype).itemsize


def _vmem_limit(pipelined, resident):
    need = 2 * sum(_nbytes(s, d) for s, d in pipelined) + sum(_nbytes(s, d) for s, d in resident)
    return int(min(VMEM_BUDGET_BYTES, max(need, 16 * 1024 * 1024)))


def _rms(x, gain):
    return (x * lax.rsqrt(jnp.mean(x * x, axis=-1, keepdims=True) + EPS)) * gain


def _cast_kernel(x_ref, o_ref):
    o_ref[...] = x_ref[...].astype(o_ref.dtype)


def _to_bf16(w):
    r, c = w.shape
    slab = r
    while slab % 32 == 0 and _nbytes((slab, c), w.dtype) > CAST_BLOCK_BYTES:
        slab //= 2
    return pl.pallas_call(
        _cast_kernel,
        name="cast",
        grid=(r // slab,),
        in_specs=[pl.BlockSpec((slab, c), lambda i: (i, 0))],
        out_specs=pl.BlockSpec((slab, c), lambda i: (i, 0)),
        out_shape=jax.ShapeDtypeStruct(w.shape, BF16),
        compiler_params=pltpu.CompilerParams(
            dimension_semantics=("parallel",), vmem_limit_bytes=VMEM_BUDGET_BYTES),
    )(w)


def _ffn_kernel(*refs, mix, apply_final_norm):
    if mix:
        (x_ref, a_ref, c_ref, wo_ref, nw_ref, wg_ref, wu_ref, wd_ref, fn_ref, o_ref,
         h_ref, res_ref) = refs
    else:
        x_ref, nw_ref, wg_ref, wu_ref, wd_ref, fn_ref, o_ref, h_ref = refs
        res_ref = x_ref
    f = pl.program_id(1)

    @pl.when(f == 0)
    def _():
        if mix:
            mixed = jnp.concatenate([a_ref[...], c_ref[...]], axis=1)
            res_ref[...] = x_ref[...] + jnp.dot(mixed, wo_ref[...], preferred_element_type=F32)
        h_ref[...] = _rms(res_ref[...], nw_ref[...]).astype(BF16)
        o_ref[...] = jnp.zeros_like(o_ref)

    h = h_ref[...]
    gate = jnp.dot(h, wg_ref[...], preferred_element_type=F32)
    up = jnp.dot(h, wu_ref[...], preferred_element_type=F32)
    act = ((gate * jax.nn.sigmoid(gate)) * up).astype(BF16)
    o_ref[...] += jnp.dot(act, wd_ref[...], preferred_element_type=F32)

    @pl.when(f == pl.num_programs(1) - 1)
    def _():
        y = res_ref[...] + 0.5 * o_ref[...]
        if apply_final_norm:
            y = _rms(y, fn_ref[...])
        o_ref[...] = y


def _ffn(x, norm_w, w_gate, w_up, w_down, final_w, apply_final_norm, mix=None):
    t, d = x.shape
    ffn = w_gate.shape[1]
    tm, tf = min(FFN_TOKEN_TILE, t), FFN_COL_TILE
    if mix is None and t == 2 * FFN_TOKEN_TILE:
        tm, tf = t, FFN_COL_TILE // 2
    assert t % tm == 0 and ffn % tf == 0
    tok = lambda i, f: (i, 0)
    const = lambda i, f: (0, 0)
    pipelined = [((tm, d), F32), ((tm, d), F32), ((d, tf), BF16), ((d, tf), BF16), ((tf, d), BF16)]
    resident = [((tm, d), BF16), ((tm, tf), F32), ((tm, tf), F32), ((tm, tf), F32), ((tm, d), F32)]
    x_specs, x_args, scratch = [pl.BlockSpec((tm, d), tok)], [x], [pltpu.VMEM((tm, d), BF16)]
    if mix is not None:
        attn_out, conv_out, w_out = mix
        wa, wc = attn_out.shape[1], conv_out.shape[1]
        assert wa + wc == w_out.shape[0] and w_out.shape[1] == d
        x_specs += [pl.BlockSpec((tm, wa), tok), pl.BlockSpec((tm, wc), tok),
                    pl.BlockSpec(w_out.shape, const, pipeline_mode=pl.Buffered(1))]
        x_args += [attn_out, conv_out, w_out]
        scratch.append(pltpu.VMEM((tm, d), F32))
        pipelined += [((tm, wa), BF16), ((tm, wc), BF16)]
        resident += [(w_out.shape, BF16), ((tm, d), F32), ((tm, d), F32)]
    return pl.pallas_call(
        functools.partial(_ffn_kernel, mix=mix is not None, apply_final_norm=apply_final_norm),
        name="ffn",
        grid=(t // tm, ffn // tf),
        in_specs=x_specs + [
            pl.BlockSpec((1, d), const),
            pl.BlockSpec((d, tf), lambda i, f: (0, f)),
            pl.BlockSpec((d, tf), lambda i, f: (0, f)),
            pl.BlockSpec((tf, d), lambda i, f: (f, 0)),
            pl.BlockSpec((1, d), const),
        ],
        out_specs=pl.BlockSpec((tm, d), tok),
        out_shape=jax.ShapeDtypeStruct((t, d), F32),
        scratch_shapes=scratch,
        compiler_params=pltpu.CompilerParams(
            dimension_semantics=("parallel", "arbitrary"),
            vmem_limit_bytes=_vmem_limit(pipelined, resident)),
    )(*x_args, norm_w, w_gate, w_up, w_down, final_w)


def _inproj_kernel(x_ref, nw_ref, w_ref, cos_ref, sin_lo_ref, sin_hi_ref, cw_ref, st_ref,
                   q_ref, k_ref, v_ref, co_ref, sto_ref, carry_ref,
                   *, width, n_heads, n_seq, tiles_per_seq, q_scale):
    i = pl.program_id(0)
    tm = x_ref.shape[0]
    seq_rows = tm // n_seq
    h = _rms(x_ref[...], nw_ref[...]).astype(BF16)

    def proj(col):
        return jnp.dot(h, w_ref[:, col * width:(col + 1) * width], preferred_element_type=F32)

    def rope_store(x, out_ref, scale):
        cos, sin_lo, sin_hi = cos_ref[...], sin_lo_ref[...], sin_hi_ref[...]
        for hd in range(n_heads):
            xh = x[:, hd * V7X_LANES:(hd + 1) * V7X_LANES]
            r = (xh * cos + pltpu.roll(xh, V7X_LANES - 32, axis=1) * sin_lo
                 + pltpu.roll(xh, 32, axis=1) * sin_hi)
            if scale is not None:
                r = r * scale
            out_ref[:, hd * V7X_LANES:(hd + 1) * V7X_LANES] = r.astype(out_ref.dtype)

    rope_store(proj(0), q_ref, q_scale)
    rope_store(proj(1), k_ref, None)
    v_ref[...] = proj(2)

    cb = proj(3)
    u = proj(4) * proj(5)
    if tiles_per_seq > 1:
        @pl.when(i % tiles_per_seq == 0)
        def _():
            carry_ref[0:2, :] = st_ref[0]
    for s in range(n_seq):
        rows = slice(s * seq_rows, (s + 1) * seq_rows)
        us = u[rows]
        if tiles_per_seq > 1:
            hist0, hist1 = carry_ref[0:1, :], carry_ref[1:2, :]
        else:
            hist0, hist1 = st_ref[s, 0:1, :], st_ref[s, 1:2, :]
        row = lax.broadcasted_iota(jnp.int32, us.shape, 0)
        back1 = pltpu.roll(us, 1, axis=0)
        back2 = pltpu.roll(us, 2, axis=0)
        u1 = jnp.where(row == 0, hist1, back1)
        u2 = jnp.where(row == 0, hist0, jnp.where(row == 1, hist1, back2))
        conv = cw_ref[0:1, :] * u2 + cw_ref[1:2, :] * u1 + cw_ref[2:3, :] * us
        co_ref[rows, :] = (cb[rows] * conv).astype(co_ref.dtype)
        last = back2[0:2, :]
        sto_ref[s] = last
        if tiles_per_seq > 1:
            carry_ref[0:2, :] = last


def _inproj(x, norm_w, w_in, rope_tabs, conv_w, conv_state, seq_len, n_heads, q_scale):
    t, d = x.shape
    width = w_in.shape[1] // 6
    n_batch = conv_state.shape[0]
    hist = conv_state.shape[1]
    assert hist == 2 and seq_len >= hist and t == n_batch * seq_len
    tm = INPROJ_TOKEN_TILE
    if seq_len >= tm:
        assert seq_len % tm == 0
        n_seq, tiles_per_seq = 1, seq_len // tm
    else:
        assert tm % seq_len == 0 and seq_len % 8 == 0
        n_seq, tiles_per_seq = tm // seq_len, 1
    cos, sin_lo, sin_hi = rope_tabs
    pos_tiles = cos.shape[0] // tm
    tab_spec = pl.BlockSpec((tm, V7X_LANES), lambda i: (i % pos_tiles, 0))
    tok = lambda i: (i, 0)
    const = lambda i: (0, 0)
    st_spec = pl.BlockSpec((n_seq, hist, width), lambda i: (i // tiles_per_seq, 0, 0))
    vmem = _vmem_limit(
        pipelined=[((tm, d), F32), ((tm, width), BF16), ((tm, width), F32), ((tm, width), F32),
                   ((tm, width), BF16), ((tm, V7X_LANES), F32), ((tm, V7X_LANES), F32),
                   ((tm, V7X_LANES), F32)],
        resident=[(w_in.shape, BF16), ((tm, d), BF16)] + [((tm, width), F32)] * 8)
    return pl.pallas_call(
        functools.partial(_inproj_kernel, width=width, n_heads=n_heads, n_seq=n_seq,
                          tiles_per_seq=tiles_per_seq, q_scale=q_scale),
        name="inproj",
        grid=(t // tm,),
        in_specs=[
            pl.BlockSpec((tm, d), tok),
            pl.BlockSpec((1, d), const),
            pl.BlockSpec(w_in.shape, const, pipeline_mode=pl.Buffered(1)),
            tab_spec, tab_spec, tab_spec,
            pl.BlockSpec(conv_w.shape, const),
            st_spec,
        ],
        out_specs=[
            pl.BlockSpec((tm, width), tok),
            pl.BlockSpec((tm, width), tok),
            pl.BlockSpec((tm, width), tok),
            pl.BlockSpec((tm, width), tok),
            st_spec,
        ],
        out_shape=[
            jax.ShapeDtypeStruct((t, width), BF16),
            jax.ShapeDtypeStruct((t, width), F32),
            jax.ShapeDtypeStruct((t, width), F32),
            jax.ShapeDtypeStruct((t, width), BF16),
            jax.ShapeDtypeStruct(conv_state.shape, F32),
        ],
        scratch_shapes=[pltpu.VMEM((8, width), F32)],
        compiler_params=pltpu.CompilerParams(
            dimension_semantics=("arbitrary",), vmem_limit_bytes=vmem),
    )(x, norm_w, w_in, cos, sin_lo, sin_hi, conv_w, conv_state)


def _stack_components(q):
    lane = lax.broadcasted_iota(jnp.int32, q.shape, 1)
    half = q.shape[1] // 2
    zero = jnp.zeros_like(q)
    return jnp.concatenate([jnp.where(lane < half, q, zero), jnp.where(lane >= half, q, zero)], axis=0)


def _visible(shape, q_start, k_start):
    assert CHUNK & (CHUNK - 1) == 0
    rows = shape[1] // 2
    c = lax.broadcasted_iota(jnp.int32, shape, 1)
    q_pos = q_start + jnp.where(c >= rows, c - rows, c)
    k_pos = k_start + lax.broadcasted_iota(jnp.int32, shape, 0)
    return k_pos < (q_pos | (CHUNK - 1)) + 1


def _scores(k, q2):
    return lax.dot_general(k, q2, (((1,), (1,)), ((), ())), preferred_element_type=F32)


def _weighted_values(v, p):
    return lax.dot_general(v, p, (((0,), (0,)), ((), ())), preferred_element_type=F32)


def _lambda(lq1_ref, lk1_ref, lq2_ref, lk2_ref, lambda_init):
    s1 = jnp.sum(lq1_ref[...] * lk1_ref[...], axis=1, keepdims=True)
    s2 = jnp.sum(lq2_ref[...] * lk2_ref[...], axis=1, keepdims=True)
    return jnp.exp(s1) - jnp.exp(s2) + lambda_init


def _diff_finish(acc, l, lam, gain_col, lambda_init):
    rows = acc.shape[1] // 2
    o = acc[:, :rows] / l[:, :rows] - lam * (acc[:, rows:] / l[:, rows:])
    y = (o * lax.rsqrt(jnp.mean(o * o, axis=0, keepdims=True) + EPS)) * gain_col
    return (y * (1.0 - lambda_init)).T


def _attn_prompt_kernel(q_ref, qn_ref, k_ref, v_ref, lq1_ref, lk1_ref, lq2_ref, lk2_ref, g_ref, o_ref,
                        kb_ref, vt_ref, q2_ref, s_ref, cmax_ref, bias_ref, m_ref, acc_ref, *, lambda_init):
    i = pl.program_id(2)
    blk = q_ref.shape[0]
    e = v_ref.shape[1]
    n_blocks = kb_ref.shape[0]
    diag_slot = 2

    def scores(kb, slot, diagonal=False):
        s = _scores(kb_ref[kb], q2_ref[...])
        if diagonal:
            s = s + bias_ref[...]
        s_ref[slot] = s
        cmax_ref[slot] = jnp.max(s, axis=0, keepdims=True)

    def accumulate(kb, slot):
        m_prev = m_ref[...]
        m_new = jnp.maximum(m_prev, cmax_ref[slot])
        alpha = jnp.exp(m_prev - m_new)
        p = jnp.exp(s_ref[slot] - m_new).astype(BF16)
        acc_ref[...] = alpha * acc_ref[...] + jnp.dot(vt_ref[kb], p, preferred_element_type=F32)
        m_ref[...] = m_new

    @pl.when(i == 0)
    def _():
        bias_ref[...] = jnp.where(_visible(bias_ref.shape, 0, 0), 0.0, MASK_VALUE)
        ones = jnp.ones((ONES_ROWS, blk), BF16)
        for j in range(n_blocks):
            rows = slice(j * blk, (j + 1) * blk)
            kb_ref[j] = k_ref[rows, :].astype(BF16)
            vt_ref[j] = jnp.concatenate([v_ref[rows, :].T.astype(BF16), ones], axis=0)
        q2_ref[...] = _stack_components(q_ref[...])
        scores(0, diag_slot, diagonal=True)

    m_ref[...] = jnp.full_like(m_ref, MASK_VALUE)
    acc_ref[...] = jnp.zeros_like(acc_ref)

    def pair(j, carry):
        scores(2 * j + 1, 1)
        accumulate(2 * j, 0)
        scores(2 * j + 2, 0)
        accumulate(2 * j + 1, 1)
        return carry

    lax.fori_loop(0, jnp.maximum(i - 1, 0) // 2, pair, 0)

    @pl.when(i % 2 == 1)
    def _():
        scores(i, diag_slot, diagonal=True)
        accumulate(i - 1, 0)

    @pl.when((i >= 2) & (i % 2 == 0))
    def _():
        scores(i - 1, 1)
        accumulate(i - 2, 0)
        scores(i, diag_slot, diagonal=True)
        accumulate(i - 1, 1)

    q2_ref[...] = _stack_components(qn_ref[...])
    scores(0, 0)
    accumulate(i, diag_slot)

    lam = _lambda(lq1_ref, lk1_ref, lq2_ref, lk2_ref, lambda_init)
    o_ref[...] = _diff_finish(acc_ref[0:e, :], acc_ref[e:e + 1, :], lam, g_ref[...],
                              lambda_init).astype(o_ref.dtype)


def _attn_prompt(q, k, v, lams, gain_col, n_batch, seq_len, n_heads, lambda_init):
    t, width = q.shape
    e = width // n_heads
    blk = ATTN_BLOCK
    assert seq_len % blk == 0 and blk % CHUNK == 0 and e == V7X_LANES
    nq = seq_len // blk
    lam_spec = pl.BlockSpec(lams[0].shape, lambda b, h, i: (0, 0))
    n_blocks = seq_len // blk
    scratch = [((n_blocks, blk, e), BF16), ((n_blocks, e + ONES_ROWS, blk), BF16), ((2 * blk, e), BF16),
               ((3, blk, 2 * blk), F32), ((3, 1, 2 * blk), F32), ((blk, 2 * blk), F32), ((1, 2 * blk), F32),
               ((e + ONES_ROWS, 2 * blk), F32)]
    vmem = _vmem_limit(
        pipelined=[((blk, e), BF16)] * 3 + [((seq_len, e), F32)] * 2,
        resident=scratch + [((blk, 2 * blk), F32)] * 6)
    return pl.pallas_call(
        functools.partial(_attn_prompt_kernel, lambda_init=lambda_init),
        name="attn_prompt",
        grid=(n_batch, n_heads, nq),
        in_specs=[
            pl.BlockSpec((blk, e), lambda b, h, i: (b * nq + i, h)),
            pl.BlockSpec((blk, e), lambda b, h, i: (b * nq + jnp.minimum(i + 1, nq - 1), h)),
            pl.BlockSpec((seq_len, e), lambda b, h, i: (b, h)),
            pl.BlockSpec((seq_len, e), lambda b, h, i: (b, h)),
            lam_spec, lam_spec, lam_spec, lam_spec,
            pl.BlockSpec(gain_col.shape, lambda b, h, i: (0, 0)),
        ],
        out_specs=pl.BlockSpec((blk, e), lambda b, h, i: (b * nq + i, h)),
        out_shape=jax.ShapeDtypeStruct((t, width), BF16),
        scratch_shapes=[pltpu.VMEM(s, d) for s, d in scratch],
        compiler_params=pltpu.CompilerParams(
            dimension_semantics=("parallel", "parallel", "arbitrary"), vmem_limit_bytes=vmem),
    )(q, q, k, v, *lams, gain_col)


def _attn_sample_kernel(q_ref, kn_ref, vn_ref, kc_ref, vc_ref, lq1_ref, lk1_ref, lq2_ref, lk2_ref,
                        g_ref, o_ref, *, n_heads, past_len, lambda_init):
    rows = q_ref.shape[0]
    e = q_ref.shape[1] // n_heads
    lam = _lambda(lq1_ref, lk1_ref, lq2_ref, lk2_ref, lambda_init)
    gain_col = g_ref[...]
    visible = _visible((rows, 2 * rows), past_len, past_len)
    for hd in range(n_heads):
        cols = slice(hd * e, (hd + 1) * e)
        kc = kc_ref[pl.ds(hd, past_len, stride=n_heads), :].astype(BF16)
        vc = vc_ref[pl.ds(hd, past_len, stride=n_heads), :].astype(BF16)
        q2 = _stack_components(q_ref[:, cols])
        s_c = _scores(kc, q2)
        s_n = jnp.where(visible, _scores(kn_ref[:, cols].astype(BF16), q2), MASK_VALUE)
        m = jnp.maximum(jnp.max(s_c, axis=0, keepdims=True), jnp.max(s_n, axis=0, keepdims=True))
        p_c = jnp.exp(s_c - m)
        p_n = jnp.exp(s_n - m)
        l = jnp.sum(p_c, axis=0, keepdims=True) + jnp.sum(p_n, axis=0, keepdims=True)
        acc = (_weighted_values(vc, p_c.astype(BF16))
               + _weighted_values(vn_ref[:, cols].astype(BF16), p_n.astype(BF16)))
        o_ref[:, cols] = _diff_finish(acc, l, lam, gain_col, lambda_init).astype(o_ref.dtype)


def _attn_sample(q, k_new, v_new, k_cache, v_cache, lams, gain_col, n_batch, seq_len, n_heads,
                 lambda_init):
    t, width = q.shape
    e = width // n_heads
    past_len = k_cache.shape[0] // (n_batch * n_heads)
    assert seq_len % 8 == 0 and past_len % 8 == 0 and k_cache.shape[1] == e
    tok = lambda b: (b, 0)
    lam_spec = pl.BlockSpec(lams[0].shape, lambda b: (0, 0))
    vmem = _vmem_limit(
        pipelined=[((past_len * n_heads, e), F32)] * 2 + [((seq_len, width), F32)] * 4,
        resident=[((past_len, 2 * seq_len), F32)] * 4 + [((past_len, e), F32)] * 4)
    return pl.pallas_call(
        functools.partial(_attn_sample_kernel, n_heads=n_heads, past_len=past_len,
                          lambda_init=lambda_init),
        name="attn_sample",
        grid=(n_batch,),
        in_specs=[
            pl.BlockSpec((seq_len, width), tok),
            pl.BlockSpec((seq_len, width), tok),
            pl.BlockSpec((seq_len, width), tok),
            pl.BlockSpec((past_len * n_heads, e), tok),
            pl.BlockSpec((past_len * n_heads, e), tok),
            lam_spec, lam_spec, lam_spec, lam_spec,
            pl.BlockSpec(gain_col.shape, lambda b: (0, 0)),
        ],
        out_specs=pl.BlockSpec((seq_len, width), tok),
        out_shape=jax.ShapeDtypeStruct((t, width), BF16),
        compiler_params=pltpu.CompilerParams(
            dimension_semantics=("parallel",), vmem_limit_bytes=vmem),
    )(q, k_new, v_new, k_cache, v_cache, *lams, gain_col)


def _rope_tables(pos, head_dim, rows):
    half = head_dim // 2
    inv = ROPE_THETA ** (-jnp.arange(0, head_dim, 2, dtype=F32) / head_dim)
    ang = pos.astype(F32)[:, None] * inv[None, :]
    cos, sin = jnp.cos(ang), jnp.sin(ang)
    zero = jnp.zeros_like(sin)
    reps = V7X_LANES // head_dim
    cos_t = jnp.tile(jnp.concatenate([cos, cos], axis=1), (rows // pos.shape[0], reps))
    sin_lo = jnp.tile(jnp.concatenate([-sin, zero], axis=1), (rows // pos.shape[0], reps))
    sin_hi = jnp.tile(jnp.concatenate([zero, sin], axis=1), (rows // pos.shape[0], reps))
    return cos_t, sin_lo, sin_hi


def kernel(x_prompt, x_sample, cache_k, cache_v, state_conv, ffn1_norm, ffn1_w_gate, ffn1_w_up,
           ffn1_w_down, mix_norm, w_in, lambda_q1, lambda_k1, lambda_q2, lambda_k2, subln_gain,
           conv_w, w_out, ffn2_norm, ffn2_w_gate, ffn2_w_up, ffn2_w_down, final_norm):
    n_bp, seq_p, d = x_prompt.shape
    n_bs, seq_s, _ = x_sample.shape
    depth, _, past_len, n_heads, key_dim = cache_k.shape
    value_dim = cache_v.shape[-1]
    head_dim = key_dim // 2
    conv_ch = conv_w.shape[-1]
    attn_width = n_heads * key_dim
    assert key_dim == V7X_LANES and value_dim == V7X_LANES and attn_width == conv_ch
    q_scale = head_dim ** -0.5

    tabs_p = _rope_tables(jnp.arange(seq_p), head_dim, seq_p)
    tabs_s = _rope_tables(past_len + jnp.arange(seq_s), head_dim, max(seq_s, INPROJ_TOKEN_TILE))

    xp = x_prompt.reshape(n_bp * seq_p, d)
    xs = x_sample.reshape(n_bs * seq_s, d)
    row = lambda a: a.reshape(1, -1)
    outs = {name: [] for name in ("kp", "vp", "cp", "ks", "vs", "cs")}
    for l in range(depth):
        lambda_init = 0.8 - 0.6 * math.exp(-0.3 * l)
        last = l == depth - 1
        w1g, w1u, w1d = (_to_bf16(w[l]) for w in (ffn1_w_gate, ffn1_w_up, ffn1_w_down))
        w2g, w2u, w2d = (_to_bf16(w[l]) for w in (ffn2_w_gate, ffn2_w_up, ffn2_w_down))
        w_in_l, w_out_l = _to_bf16(w_in[l]), _to_bf16(w_out[l])
        lams = tuple(row(a[l]) for a in (lambda_q1, lambda_k1, lambda_q2, lambda_k2))
        gain = subln_gain[l].reshape(-1, 1)
        fin = row(final_norm)
        zeros_state = jnp.zeros((n_bp, conv_w.shape[1] - 1, conv_ch), F32)

        def layer(x, tabs, state, seq_len, attend):
            x = _ffn(x, row(ffn1_norm[l]), w1g, w1u, w1d, fin, False)
            q, k, v, conv_out, new_state = _inproj(
                x, row(mix_norm[l]), w_in_l, tabs, conv_w[l], state, seq_len, n_heads, q_scale)
            attn_out = attend(q, k, v)
            x = _ffn(x, row(ffn2_norm[l]), w2g, w2u, w2d, fin, last, mix=(attn_out, conv_out, w_out_l))
            return x, k, v, new_state

        xp, kp, vp, cp = layer(
            xp, tabs_p, zeros_state, seq_p,
            lambda q, k, v: _attn_prompt(q, k, v, lams, gain, n_bp, seq_p, n_heads, lambda_init))
        kc = cache_k[l].reshape(n_bs * past_len * n_heads, key_dim)
        vc = cache_v[l].reshape(n_bs * past_len * n_heads, value_dim)
        xs, ks, vs, cs = layer(
            xs, tabs_s, state_conv[l], seq_s,
            lambda q, k, v: _attn_sample(q, k, v, kc, vc, lams, gain, n_bs, seq_s, n_heads,
                                         lambda_init))
        outs["kp"].append(kp.reshape(n_bp, seq_p, n_heads, key_dim))
        outs["vp"].append(vp.reshape(n_bp, seq_p, n_heads, value_dim))
        outs["cp"].append(cp)
        outs["ks"].append(ks.reshape(n_bs, seq_s, n_heads, key_dim))
        outs["vs"].append(vs.reshape(n_bs, seq_s, n_heads, value_dim))
        outs["cs"].append(cs)
    return (xp.reshape(x_prompt.shape), xs.reshape(x_sample.shape),
            jnp.stack(outs["kp"]), jnp.stack(outs["vp"]), jnp.stack(outs["cp"]),
            jnp.stack(outs["ks"]), jnp.stack(outs["vs"]), jnp.stack(outs["cs"]))
```

```python
import functools
import math

import jax
import jax.numpy as jnp
from jax import lax
from jax.experimental import pallas as pl
from jax.experimental.pallas import tpu as pltpu

F32 = jnp.float32
BF16 = jnp.bfloat16

CHUNK = 64
ROPE_THETA = 10000.0
EPS = 1e-6
MASK_VALUE = -1e30

V7X_LANES = 128
V7X_VMEM_BYTES = 64 * 1024 * 1024
VMEM_BUDGET_BYTES = V7X_VMEM_BYTES * 7 // 8

CAST_BLOCK_BYTES = 6 * 1024 * 1024
FFN_TOKEN_TILE = 512
FFN_COL_TILE = 512
INPROJ_TOKEN_TILE = 256
ATTN_BLOCK = 512
ONES_ROWS = 16
assert CHUNK & (CHUNK - 1) == 0


def _nbytes(shape, dtype):
    return math.prod(shape) * jnp.dtype(dtype).itemsize


def _vmem_limit(pipelined, resident):
    need = 2 * sum(_nbytes(s, d) for s, d in pipelined) + sum(_nbytes(s, d) for s, d in resident)
    return int(min(VMEM_BUDGET_BYTES, max(need, 16 * 1024 * 1024)))


def _rms(x, gain):
    return (x * lax.rsqrt(jnp.mean(x * x, axis=-1, keepdims=True) + EPS)) * gain


def _cast_kernel(x_ref, o_ref):
    o_ref[...] = x_ref[...].astype(o_ref.dtype)


def _to_bf16(w):
    r, c = w.shape
    slab = r
    while slab % 32 == 0 and _nbytes((slab, c), w.dtype) > CAST_BLOCK_BYTES:
        slab //= 2
    return pl.pallas_call(
        _cast_kernel,
        name="cast",
        grid=(r // slab,),
        in_specs=[pl.BlockSpec((slab, c), lambda i: (i, 0))],
        out_specs=pl.BlockSpec((slab, c), lambda i: (i, 0)),
        out_shape=jax.ShapeDtypeStruct(w.shape, BF16),
        compiler_params=pltpu.CompilerParams(
            dimension_semantics=("parallel",), vmem_limit_bytes=VMEM_BUDGET_BYTES),
    )(w)


def _ffn_kernel(*refs, mix, apply_final_norm):
    if mix:
        (x_ref, a_ref, c_ref, wo_ref, nw_ref, wg_ref, wu_ref, wd_ref, fn_ref, o_ref,
         h_ref, res_ref) = refs
    else:
        x_ref, nw_ref, wg_ref, wu_ref, wd_ref, fn_ref, o_ref, h_ref = refs
        res_ref = x_ref
    f = pl.program_id(1)

    @pl.when(f == 0)
    def _():
        if mix:
            mixed = jnp.concatenate([a_ref[...], c_ref[...]], axis=1)
            res_ref[...] = x_ref[...] + jnp.dot(mixed, wo_ref[...], preferred_element_type=F32)
        h_ref[...] = _rms(res_ref[...], nw_ref[...]).astype(BF16)
        o_ref[...] = jnp.zeros_like(o_ref)

    h = h_ref[...]
    gate = jnp.dot(h, wg_ref[...], preferred_element_type=F32)
    up = jnp.dot(h, wu_ref[...], preferred_element_type=F32)
    act = ((gate * jax.nn.sigmoid(gate)) * up).astype(BF16)
    o_ref[...] += jnp.dot(act, wd_ref[...], preferred_element_type=F32)

    @pl.when(f == pl.num_programs(1) - 1)
    def _():
        y = res_ref[...] + 0.5 * o_ref[...]
        if apply_final_norm:
            y = _rms(y, fn_ref[...])
        o_ref[...] = y


def _ffn(x, norm_w, w_gate, w_up, w_down, final_w, apply_final_norm, mix=None):
    t, d = x.shape
    ffn = w_gate.shape[1]
    tm, tf = min(FFN_TOKEN_TILE, t), FFN_COL_TILE
    if mix is None and t == 2 * FFN_TOKEN_TILE:
        tm, tf = t, FFN_COL_TILE // 2
    assert t % tm == 0 and ffn % tf == 0
    tok = lambda i, f: (i, 0)
    const = lambda i, f: (0, 0)
    pipelined = [((tm, d), F32), ((tm, d), F32), ((d, tf), BF16), ((d, tf), BF16), ((tf, d), BF16)]
    resident = [((tm, d), BF16), ((tm, tf), F32), ((tm, tf), F32), ((tm, tf), F32), ((tm, d), F32)]
    x_specs, x_args, scratch = [pl.BlockSpec((tm, d), tok)], [x], [pltpu.VMEM((tm, d), BF16)]
    if mix is not None:
        attn_out, conv_out, w_out = mix
        wa, wc = attn_out.shape[1], conv_out.shape[1]
        assert wa + wc == w_out.shape[0] and w_out.shape[1] == d
        x_specs += [pl.BlockSpec((tm, wa), tok), pl.BlockSpec((tm, wc), tok),
                    pl.BlockSpec(w_out.shape, const, pipeline_mode=pl.Buffered(1))]
        x_args += [attn_out, conv_out, w_out]
        scratch.append(pltpu.VMEM((tm, d), F32))
        pipelined += [((tm, wa), BF16), ((tm, wc), BF16)]
        resident += [(w_out.shape, BF16), ((tm, d), F32), ((tm, d), F32)]
    return pl.pallas_call(
        functools.partial(_ffn_kernel, mix=mix is not None, apply_final_norm=apply_final_norm),
        name="ffn",
        grid=(t // tm, ffn // tf),
        in_specs=x_specs + [
            pl.BlockSpec((1, d), const),
            pl.BlockSpec((d, tf), lambda i, f: (0, f)),
            pl.BlockSpec((d, tf), lambda i, f: (0, f)),
            pl.BlockSpec((tf, d), lambda i, f: (f, 0)),
            pl.BlockSpec((1, d), const),
        ],
        out_specs=pl.BlockSpec((tm, d), tok),
        out_shape=jax.ShapeDtypeStruct((t, d), F32),
        scratch_shapes=scratch,
        compiler_params=pltpu.CompilerParams(
            dimension_semantics=("parallel", "arbitrary"),
            vmem_limit_bytes=_vmem_limit(pipelined, resident)),
    )(*x_args, norm_w, w_gate, w_up, w_down, final_w)


def _inproj_kernel(x_ref, nw_ref, w_ref, cos_ref, sin_lo_ref, sin_hi_ref, cw_ref, st_ref,
                   q_ref, k_ref, v_ref, co_ref, sto_ref, carry_ref,
                   *, width, n_heads, n_seq, tiles_per_seq, q_scale):
    i = pl.program_id(0)
    tm = x_ref.shape[0]
    seq_rows = tm // n_seq
    h = _rms(x_ref[...], nw_ref[...]).astype(BF16)

    def proj(col):
        return jnp.dot(h, w_ref[:, col * width:(col + 1) * width], preferred_element_type=F32)

    def rope_store(x, out_ref, scale):
        cos, sin_lo, sin_hi = cos_ref[...], sin_lo_ref[...], sin_hi_ref[...]
        for hd in range(n_heads):
            xh = x[:, hd * V7X_LANES:(hd + 1) * V7X_LANES]
            r = (xh * cos + pltpu.roll(xh, V7X_LANES - 32, axis=1) * sin_lo
                 + pltpu.roll(xh, 32, axis=1) * sin_hi)
            if scale is not None:
                r = r * scale
            out_ref[:, hd * V7X_LANES:(hd + 1) * V7X_LANES] = r.astype(out_ref.dtype)

    rope_store(proj(0), q_ref, q_scale)
    rope_store(proj(1), k_ref, None)
    v_ref[...] = proj(2)

    cb = proj(3)
    u = proj(4) * proj(5)
    if tiles_per_seq > 1:
        @pl.when(i % tiles_per_seq == 0)
        def _():
            carry_ref[0:2, :] = st_ref[0]
    for s in range(n_seq):
        rows = slice(s * seq_rows, (s + 1) * seq_rows)
        us = u[rows]
        if tiles_per_seq > 1:
            hist0, hist1 = carry_ref[0:1, :], carry_ref[1:2, :]
        else:
            hist0, hist1 = st_ref[s, 0:1, :], st_ref[s, 1:2, :]
        row = lax.broadcasted_iota(jnp.int32, us.shape, 0)
        back1 = pltpu.roll(us, 1, axis=0)
        back2 = pltpu.roll(us, 2, axis=0)
        u1 = jnp.where(row == 0, hist1, back1)
        u2 = jnp.where(row == 0, hist0, jnp.where(row == 1, hist1, back2))
        conv = cw_ref[0:1, :] * u2 + cw_ref[1:2, :] * u1 + cw_ref[2:3, :] * us
        co_ref[rows, :] = (cb[rows] * conv).astype(co_ref.dtype)
        last = back2[0:2, :]
        sto_ref[s] = last
        if tiles_per_seq > 1:
            carry_ref[0:2, :] = last


def _inproj(x, norm_w, w_in, rope_tabs, conv_w, conv_state, seq_len, n_heads, q_scale):
    t, d = x.shape
    width = w_in.shape[1] // 6
    n_batch = conv_state.shape[0]
    hist = conv_state.shape[1]
    assert hist == 2 and seq_len >= hist and t == n_batch * seq_len
    tm = INPROJ_TOKEN_TILE
    if seq_len >= tm:
        assert seq_len % tm == 0
        n_seq, tiles_per_seq = 1, seq_len // tm
    else:
        assert tm % seq_len == 0 and seq_len % 8 == 0
        n_seq, tiles_per_seq = tm // seq_len, 1
    cos, sin_lo, sin_hi = rope_tabs
    pos_tiles = cos.shape[0] // tm
    tab_spec = pl.BlockSpec((tm, V7X_LANES), lambda i: (i % pos_tiles, 0))
    tok = lambda i: (i, 0)
    const = lambda i: (0, 0)
    st_spec = pl.BlockSpec((n_seq, hist, width), lambda i: (i // tiles_per_seq, 0, 0))
    vmem = _vmem_limit(
        pipelined=[((tm, d), F32), ((tm, width), BF16), ((tm, width), F32), ((tm, width), F32),
                   ((tm, width), BF16), ((tm, V7X_LANES), F32), ((tm, V7X_LANES), F32),
                   ((tm, V7X_LANES), F32)],
        resident=[(w_in.shape, BF16), ((tm, d), BF16)] + [((tm, width), F32)] * 8)
    return pl.pallas_call(
        functools.partial(_inproj_kernel, width=width, n_heads=n_heads, n_seq=n_seq,
                          tiles_per_seq=tiles_per_seq, q_scale=q_scale),
        name="inproj",
        grid=(t // tm,),
        in_specs=[
            pl.BlockSpec((tm, d), tok),
            pl.BlockSpec((1, d), const),
            pl.BlockSpec(w_in.shape, const, pipeline_mode=pl.Buffered(1)),
            tab_spec, tab_spec, tab_spec,
            pl.BlockSpec(conv_w.shape, const),
            st_spec,
        ],
        out_specs=[
            pl.BlockSpec((tm, width), tok),
            pl.BlockSpec((tm, width), tok),
            pl.BlockSpec((tm, width), tok),
            pl.BlockSpec((tm, width), tok),
            st_spec,
        ],
        out_shape=[
            jax.ShapeDtypeStruct((t, width), BF16),
            jax.ShapeDtypeStruct((t, width), F32),
            jax.ShapeDtypeStruct((t, width), F32),
            jax.ShapeDtypeStruct((t, width), BF16),
            jax.ShapeDtypeStruct(conv_state.shape, F32),
        ],
        scratch_shapes=[pltpu.VMEM((8, width), F32)],
        compiler_params=pltpu.CompilerParams(
            dimension_semantics=("arbitrary",), vmem_limit_bytes=vmem),
    )(x, norm_w, w_in, cos, sin_lo, sin_hi, conv_w, conv_state)


def _stack_components(q):
    lane = lax.broadcasted_iota(jnp.int32, q.shape, 1)
    half = q.shape[1] // 2
    zero = jnp.zeros_like(q)
    return jnp.concatenate([jnp.where(lane < half, q, zero), jnp.where(lane >= half, q, zero)], axis=0)


def _visible(shape, q_start, k_start):
    assert CHUNK & (CHUNK - 1) == 0
    rows = shape[1] // 2
    c = lax.broadcasted_iota(jnp.int32, shape, 1)
    q_pos = q_start + jnp.where(c >= rows, c - rows, c)
    k_pos = k_start + lax.broadcasted_iota(jnp.int32, shape, 0)
    return k_pos < (q_pos | (CHUNK - 1)) + 1


def _scores(k, q2):
    return lax.dot_general(k, q2, (((1,), (1,)), ((), ())), preferred_element_type=F32)


def _weighted_values(v, p):
    return lax.dot_general(v, p, (((0,), (0,)), ((), ())), preferred_element_type=F32)


def _lambda(lq1_ref, lk1_ref, lq2_ref, lk2_ref, lambda_init):
    s1 = jnp.sum(lq1_ref[...] * lk1_ref[...], axis=1, keepdims=True)
    s2 = jnp.sum(lq2_ref[...] * lk2_ref[...], axis=1, keepdims=True)
    return jnp.exp(s1) - jnp.exp(s2) + lambda_init


def _diff_finish(acc, l, lam, gain_col, lambda_init):
    rows = acc.shape[1] // 2
    o = acc[:, :rows] / l[:, :rows] - lam * (acc[:, rows:] / l[:, rows:])
    y = (o * lax.rsqrt(jnp.mean(o * o, axis=0, keepdims=True) + EPS)) * gain_col
    return (y * (1.0 - lambda_init)).T


def _attn_prompt_kernel(q_ref, qn_ref, k_ref, v_ref, lq1_ref, lk1_ref, lq2_ref, lk2_ref, g_ref, o_ref,
                        kb_ref, vt_ref, q2_ref, s_ref, cmax_ref, bias_ref, m_ref, acc_ref, *, lambda_init):
    i = pl.program_id(2)
    blk = q_ref.shape[0]
    e = v_ref.shape[1]
    n_blocks = kb_ref.shape[0]
    diag_slot = 2

    def scores(kb, slot, diagonal=False):
        s = _scores(kb_ref[kb], q2_ref[...])
        if diagonal:
            s = s + bias_ref[...]
        s_ref[slot] = s
        cmax_ref[slot] = jnp.max(s, axis=0, keepdims=True)

    def accumulate(kb, slot):
        m_prev = m_ref[...]
        m_new = jnp.maximum(m_prev, cmax_ref[slot])
        alpha = jnp.exp(m_prev - m_new)
        p = jnp.exp(s_ref[slot] - m_new).astype(BF16)
        acc_ref[...] = alpha * acc_ref[...] + jnp.dot(vt_ref[kb], p, preferred_element_type=F32)
        m_ref[...] = m_new

    m_ref[...] = jnp.full_like(m_ref, MASK_VALUE)
    acc_ref[...] = jnp.zeros_like(acc_ref)

    def finish():
        q2_ref[...] = _stack_components(qn_ref[...])
        scores(0, 0)
        accumulate(i, diag_slot)
        lam = _lambda(lq1_ref, lk1_ref, lq2_ref, lk2_ref, lambda_init)
        o_ref[...] = _diff_finish(acc_ref[0:e, :], acc_ref[e:e + 1, :], lam, g_ref[...],
                                  lambda_init).astype(o_ref.dtype)

    @pl.when((pl.program_id(0) == 0) & (pl.program_id(1) == 0) & (i == 0))
    def _():
        bias_ref[...] = jnp.where(_visible(bias_ref.shape, 0, 0), 0.0, MASK_VALUE)

    @pl.when(i == 0)
    def _():
        ones = jnp.ones((ONES_ROWS, blk), BF16)
        for j in range(n_blocks):
            rows = slice(j * blk, (j + 1) * blk)
            kb_ref[j] = k_ref[rows, :].astype(BF16)
            vt_ref[j] = jnp.concatenate([v_ref[rows, :].T.astype(BF16), ones], axis=0)
        q2_ref[...] = _stack_components(q_ref[...])
        scores(0, diag_slot, diagonal=True)
        finish()

    def pair(j, carry):
        scores(2 * j + 1, 1)
        accumulate(2 * j, 0)
        scores(2 * j + 2, 0)
        accumulate(2 * j + 1, 1)
        return carry

    lax.fori_loop(0, jnp.maximum(i - 1, 0) // 2, pair, 0)

    @pl.when(i % 2 == 1)
    def _():
        scores(i, diag_slot, diagonal=True)
        accumulate(i - 1, 0)
        finish()

    @pl.when((i >= 2) & (i % 2 == 0))
    def _():
        scores(i - 1, 1)
        accumulate(i - 2, 0)
        scores(i, diag_slot, diagonal=True)
        accumulate(i - 1, 1)
        finish()


def _attn_prompt(q, k, v, lams, gain_col, n_batch, seq_len, n_heads, lambda_init):
    t, width = q.shape
    e = width // n_heads
    blk = ATTN_BLOCK
    assert seq_len % blk == 0 and blk % CHUNK == 0 and e == V7X_LANES
    nq = seq_len // blk
    lam_spec = pl.BlockSpec(lams[0].shape, lambda b, h, i: (0, 0))
    n_blocks = seq_len // blk
    scratch = [((n_blocks, blk, e), BF16), ((n_blocks, e + ONES_ROWS, blk), BF16), ((2 * blk, e), BF16),
               ((3, blk, 2 * blk), F32), ((3, 1, 2 * blk), F32), ((blk, 2 * blk), F32), ((1, 2 * blk), F32),
               ((e + ONES_ROWS, 2 * blk), F32)]
    vmem = _vmem_limit(
        pipelined=[((blk, e), BF16)] * 3 + [((seq_len, e), F32)] * 2,
        resident=scratch + [((blk, 2 * blk), F32)] * 6)
    return pl.pallas_call(
        functools.partial(_attn_prompt_kernel, lambda_init=lambda_init),
        name="attn_prompt",
        grid=(n_batch, n_heads, nq),
        in_specs=[
            pl.BlockSpec((blk, e), lambda b, h, i: (b * nq + i, h)),
            pl.BlockSpec((blk, e), lambda b, h, i: (b * nq + jnp.minimum(i + 1, nq - 1), h)),
            pl.BlockSpec((seq_len, e), lambda b, h, i: (b, h)),
            pl.BlockSpec((seq_len, e), lambda b, h, i: (b, h)),
            lam_spec, lam_spec, lam_spec, lam_spec,
            pl.BlockSpec(gain_col.shape, lambda b, h, i: (0, 0)),
        ],
        out_specs=pl.BlockSpec((blk, e), lambda b, h, i: (b * nq + i, h)),
        out_shape=jax.ShapeDtypeStruct((t, width), BF16),
        scratch_shapes=[pltpu.VMEM(s, d) for s, d in scratch],
        compiler_params=pltpu.CompilerParams(
            dimension_semantics=("arbitrary", "arbitrary", "arbitrary"), vmem_limit_bytes=vmem),
    )(q, q, k, v, *lams, gain_col)


def _attn_sample_kernel(q_ref, kn_ref, vn_ref, kc_ref, vc_ref, lq1_ref, lk1_ref, lq2_ref, lk2_ref,
                        g_ref, o_ref, *, n_heads, past_len, lambda_init):
    rows = q_ref.shape[0]
    e = q_ref.shape[1] // n_heads
    lam = _lambda(lq1_ref, lk1_ref, lq2_ref, lk2_ref, lambda_init)
    gain_col = g_ref[...]
    visible = _visible((rows, 2 * rows), past_len, past_len)
    for hd in range(n_heads):
        cols = slice(hd * e, (hd + 1) * e)
        kc = kc_ref[pl.ds(hd, past_len, stride=n_heads), :].astype(BF16)
        vc = vc_ref[pl.ds(hd, past_len, stride=n_heads), :].astype(BF16)
        q2 = _stack_components(q_ref[:, cols])
        s_c = _scores(kc, q2)
        s_n = jnp.where(visible, _scores(kn_ref[:, cols].astype(BF16), q2), MASK_VALUE)
        m = jnp.maximum(jnp.max(s_c, axis=0, keepdims=True), jnp.max(s_n, axis=0, keepdims=True))
        p_c = jnp.exp(s_c - m)
        p_n = jnp.exp(s_n - m)
        l = jnp.sum(p_c, axis=0, keepdims=True) + jnp.sum(p_n, axis=0, keepdims=True)
        acc = (_weighted_values(vc, p_c.astype(BF16))
               + _weighted_values(vn_ref[:, cols].astype(BF16), p_n.astype(BF16)))
        o_ref[:, cols] = _diff_finish(acc, l, lam, gain_col, lambda_init).astype(o_ref.dtype)


def _attn_sample(q, k_new, v_new, k_cache, v_cache, lams, gain_col, n_batch, seq_len, n_heads,
                 lambda_init):
    t, width = q.shape
    e = width // n_heads
    past_len = k_cache.shape[0] // (n_batch * n_heads)
    assert seq_len % 8 == 0 and past_len % 8 == 0 and k_cache.shape[1] == e
    tok = lambda b: (b, 0)
    lam_spec = pl.BlockSpec(lams[0].shape, lambda b: (0, 0))
    vmem = _vmem_limit(
        pipelined=[((past_len * n_heads, e), F32)] * 2 + [((seq_len, width), F32)] * 4,
        resident=[((past_len, 2 * seq_len), F32)] * 4 + [((past_len, e), F32)] * 4)
    return pl.pallas_call(
        functools.partial(_attn_sample_kernel, n_heads=n_heads, past_len=past_len,
                          lambda_init=lambda_init),
        name="attn_sample",
        grid=(n_batch,),
        in_specs=[
            pl.BlockSpec((seq_len, width), tok),
            pl.BlockSpec((seq_len, width), tok),
            pl.BlockSpec((seq_len, width), tok),
            pl.BlockSpec((past_len * n_heads, e), tok),
            pl.BlockSpec((past_len * n_heads, e), tok),
            lam_spec, lam_spec, lam_spec, lam_spec,
            pl.BlockSpec(gain_col.shape, lambda b: (0, 0)),
        ],
        out_specs=pl.BlockSpec((seq_len, width), tok),
        out_shape=jax.ShapeDtypeStruct((t, width), BF16),
        compiler_params=pltpu.CompilerParams(
            dimension_semantics=("parallel",), vmem_limit_bytes=vmem),
    )(q, k_new, v_new, k_cache, v_cache, *lams, gain_col)


def _rope_tables(pos, head_dim, rows):
    half = head_dim // 2
    inv = ROPE_THETA ** (-jnp.arange(0, head_dim, 2, dtype=F32) / head_dim)
    ang = pos.astype(F32)[:, None] * inv[None, :]
    cos, sin = jnp.cos(ang), jnp.sin(ang)
    zero = jnp.zeros_like(sin)
    reps = V7X_LANES // head_dim
    cos_t = jnp.tile(jnp.concatenate([cos, cos], axis=1), (rows // pos.shape[0], reps))
    sin_lo = jnp.tile(jnp.concatenate([-sin, zero], axis=1), (rows // pos.shape[0], reps))
    sin_hi = jnp.tile(jnp.concatenate([zero, sin], axis=1), (rows // pos.shape[0], reps))
    return cos_t, sin_lo, sin_hi


def kernel(x_prompt, x_sample, cache_k, cache_v, state_conv, ffn1_norm, ffn1_w_gate, ffn1_w_up,
           ffn1_w_down, mix_norm, w_in, lambda_q1, lambda_k1, lambda_q2, lambda_k2, subln_gain,
           conv_w, w_out, ffn2_norm, ffn2_w_gate, ffn2_w_up, ffn2_w_down, final_norm):
    n_bp, seq_p, d = x_prompt.shape
    n_bs, seq_s, _ = x_sample.shape
    depth, _, past_len, n_heads, key_dim = cache_k.shape
    value_dim = cache_v.shape[-1]
    head_dim = key_dim // 2
    conv_ch = conv_w.shape[-1]
    attn_width = n_heads * key_dim
    assert key_dim == V7X_LANES and value_dim == V7X_LANES and attn_width == conv_ch
    q_scale = head_dim ** -0.5

    tabs_p = _rope_tables(jnp.arange(seq_p), head_dim, seq_p)
    tabs_s = _rope_tables(past_len + jnp.arange(seq_s), head_dim, max(seq_s, INPROJ_TOKEN_TILE))

    xp = x_prompt.reshape(n_bp * seq_p, d)
    xs = x_sample.reshape(n_bs * seq_s, d)
    row = lambda a: a.reshape(1, -1)
    outs = {name: [] for name in ("kp", "vp", "cp", "ks", "vs", "cs")}
    for l in range(depth):
        lambda_init = 0.8 - 0.6 * math.exp(-0.3 * l)
        last = l == depth - 1
        w1g, w1u, w1d = (_to_bf16(w[l]) for w in (ffn1_w_gate, ffn1_w_up, ffn1_w_down))
        w2g, w2u, w2d = (_to_bf16(w[l]) for w in (ffn2_w_gate, ffn2_w_up, ffn2_w_down))
        w_in_l, w_out_l = _to_bf16(w_in[l]), _to_bf16(w_out[l])
        lams = tuple(row(a[l]) for a in (lambda_q1, lambda_k1, lambda_q2, lambda_k2))
        gain = subln_gain[l].reshape(-1, 1)
        fin = row(final_norm)
        zeros_state = jnp.zeros((n_bp, conv_w.shape[1] - 1, conv_ch), F32)

        def layer(x, tabs, state, seq_len, attend):
            x = _ffn(x, row(ffn1_norm[l]), w1g, w1u, w1d, fin, False)
            q, k, v, conv_out, new_state = _inproj(
                x, row(mix_norm[l]), w_in_l, tabs, conv_w[l], state, seq_len, n_heads, q_scale)
            attn_out = attend(q, k, v)
            x = _ffn(x, row(ffn2_norm[l]), w2g, w2u, w2d, fin, last, mix=(attn_out, conv_out, w_out_l))
            return x, k, v, new_state

        xp, kp, vp, cp = layer(
            xp, tabs_p, zeros_state, seq_p,
            lambda q, k, v: _attn_prompt(q, k, v, lams, gain, n_bp, seq_p, n_heads, lambda_init))
        kc = cache_k[l].reshape(n_bs * past_len * n_heads, key_dim)
        vc = cache_v[l].reshape(n_bs * past_len * n_heads, value_dim)
        xs, ks, vs, cs = layer(
            xs, tabs_s, state_conv[l], seq_s,
            lambda q, k, v: _attn_sample(q, k, v, kc, vc, lams, gain, n_bs, seq_s, n_heads,
                                         lambda_init))
        outs["kp"].append(kp.reshape(n_bp, seq_p, n_heads, key_dim))
        outs["vp"].append(vp.reshape(n_bp, seq_p, n_heads, value_dim))
        outs["cp"].append(cp)
        outs["ks"].append(ks.reshape(n_bs, seq_s, n_heads, key_dim))
        outs["vs"].append(vs.reshape(n_bs, seq_s, n_heads, value_dim))
        outs["cs"].append(cs)
    return (xp.reshape(x_prompt.shape), xs.reshape(x_sample.shape),
            jnp.stack(outs["kp"]), jnp.stack(outs["vp"]), jnp.stack(outs["cp"]),
            jnp.stack(outs["ks"]), jnp.stack(outs["vs"]), jnp.stack(outs["cs"]))
```

```python
import functools
import math

import jax
import jax.numpy as jnp
from jax import lax
from jax.experimental import pallas as pl
from jax.experimental.pallas import tpu as pltpu

F32 = jnp.float32
BF16 = jnp.bfloat16

CHUNK = 64
ROPE_THETA = 10000.0
EPS = 1e-6
MASK_VALUE = -1e30

V7X_LANES = 128
V7X_VMEM_BYTES = 64 * 1024 * 1024
VMEM_BUDGET_BYTES = V7X_VMEM_BYTES * 7 // 8

CAST_BLOCK_BYTES = 6 * 1024 * 1024
FFN_TOKEN_TILE = 512
FFN_COL_TILE = 512
INPROJ_TOKEN_TILE = 256
ATTN_BLOCK = 512
ATTN_LOOP_UNROLL = 4
ONES_ROWS = 16
assert CHUNK & (CHUNK - 1) == 0


def _nbytes(shape, dtype):
    return math.prod(shape) * jnp.dtype(dtype).itemsize


def _vmem_limit(pipelined, resident):
    need = 2 * sum(_nbytes(s, d) for s, d in pipelined) + sum(_nbytes(s, d) for s, d in resident)
    return int(min(VMEM_BUDGET_BYTES, max(need, 16 * 1024 * 1024)))


def _rms(x, gain):
    return (x * lax.rsqrt(jnp.mean(x * x, axis=-1, keepdims=True) + EPS)) * gain


def _cast_kernel(x_ref, o_ref):
    o_ref[...] = x_ref[...].astype(o_ref.dtype)


def _to_bf16(w):
    r, c = w.shape
    slab = r
    while slab % 32 == 0 and _nbytes((slab, c), w.dtype) > CAST_BLOCK_BYTES:
        slab //= 2
    return pl.pallas_call(
        _cast_kernel,
        name="cast",
        grid=(r // slab,),
        in_specs=[pl.BlockSpec((slab, c), lambda i: (i, 0))],
        out_specs=pl.BlockSpec((slab, c), lambda i: (i, 0)),
        out_shape=jax.ShapeDtypeStruct(w.shape, BF16),
        compiler_params=pltpu.CompilerParams(
            dimension_semantics=("parallel",), vmem_limit_bytes=VMEM_BUDGET_BYTES),
    )(w)


def _ffn_kernel(*refs, mix, apply_final_norm):
    if mix:
        (x_ref, a_ref, c_ref, wo_ref, nw_ref, wg_ref, wu_ref, wd_ref, fn_ref, o_ref,
         h_ref, res_ref) = refs
    else:
        x_ref, nw_ref, wg_ref, wu_ref, wd_ref, fn_ref, o_ref, h_ref = refs
        res_ref = x_ref
    f = pl.program_id(1)

    @pl.when(f == 0)
    def _():
        if mix:
            mixed = jnp.concatenate([a_ref[...], c_ref[...]], axis=1)
            res_ref[...] = x_ref[...] + jnp.dot(mixed, wo_ref[...], preferred_element_type=F32)
        h_ref[...] = _rms(res_ref[...], nw_ref[...]).astype(BF16)
        o_ref[...] = jnp.zeros_like(o_ref)

    h = h_ref[...]
    gate = jnp.dot(h, wg_ref[...], preferred_element_type=F32)
    up = jnp.dot(h, wu_ref[...], preferred_element_type=F32)
    act = ((gate * jax.nn.sigmoid(gate)) * up).astype(BF16)
    o_ref[...] += jnp.dot(act, wd_ref[...], preferred_element_type=F32)

    @pl.when(f == pl.num_programs(1) - 1)
    def _():
        y = res_ref[...] + 0.5 * o_ref[...]
        if apply_final_norm:
            y = _rms(y, fn_ref[...])
        o_ref[...] = y


def _ffn(x, norm_w, w_gate, w_up, w_down, final_w, apply_final_norm, mix=None):
    t, d = x.shape
    ffn = w_gate.shape[1]
    tm, tf = min(FFN_TOKEN_TILE, t), FFN_COL_TILE
    if mix is None and t == 2 * FFN_TOKEN_TILE:
        tm, tf = t, FFN_COL_TILE // 2
    assert t % tm == 0 and ffn % tf == 0
    tok = lambda i, f: (i, 0)
    const = lambda i, f: (0, 0)
    pipelined = [((tm, d), F32), ((tm, d), F32), ((d, tf), BF16), ((d, tf), BF16), ((tf, d), BF16)]
    resident = [((tm, d), BF16), ((tm, tf), F32), ((tm, tf), F32), ((tm, tf), F32), ((tm, d), F32)]
    x_specs, x_args, scratch = [pl.BlockSpec((tm, d), tok)], [x], [pltpu.VMEM((tm, d), BF16)]
    if mix is not None:
        attn_out, conv_out, w_out = mix
        wa, wc = attn_out.shape[1], conv_out.shape[1]
        assert wa + wc == w_out.shape[0] and w_out.shape[1] == d
        x_specs += [pl.BlockSpec((tm, wa), tok), pl.BlockSpec((tm, wc), tok),
                    pl.BlockSpec(w_out.shape, const, pipeline_mode=pl.Buffered(1))]
        x_args += [attn_out, conv_out, w_out]
        scratch.append(pltpu.VMEM((tm, d), F32))
        pipelined += [((tm, wa), BF16), ((tm, wc), BF16)]
        resident += [(w_out.shape, BF16), ((tm, d), F32), ((tm, d), F32)]
    return pl.pallas_call(
        functools.partial(_ffn_kernel, mix=mix is not None, apply_final_norm=apply_final_norm),
        name="ffn",
        grid=(t // tm, ffn // tf),
        in_specs=x_specs + [
            pl.BlockSpec((1, d), const),
            pl.BlockSpec((d, tf), lambda i, f: (0, f)),
            pl.BlockSpec((d, tf), lambda i, f: (0, f)),
            pl.BlockSpec((tf, d), lambda i, f: (f, 0)),
            pl.BlockSpec((1, d), const),
        ],
        out_specs=pl.BlockSpec((tm, d), tok),
        out_shape=jax.ShapeDtypeStruct((t, d), F32),
        scratch_shapes=scratch,
        compiler_params=pltpu.CompilerParams(
            dimension_semantics=("parallel", "arbitrary"),
            vmem_limit_bytes=_vmem_limit(pipelined, resident)),
    )(*x_args, norm_w, w_gate, w_up, w_down, final_w)


def _inproj_kernel(x_ref, nw_ref, w_ref, cos_ref, sin_lo_ref, sin_hi_ref, cw_ref, st_ref,
                   q_ref, k_ref, v_ref, co_ref, sto_ref, carry_ref,
                   *, width, n_heads, n_seq, tiles_per_seq, q_scale):
    i = pl.program_id(0)
    tm = x_ref.shape[0]
    seq_rows = tm // n_seq
    h = _rms(x_ref[...], nw_ref[...]).astype(BF16)

    def proj(col):
        return jnp.dot(h, w_ref[:, col * width:(col + 1) * width], preferred_element_type=F32)

    def rope_store(x, out_ref, scale):
        cos, sin_lo, sin_hi = cos_ref[...], sin_lo_ref[...], sin_hi_ref[...]
        for hd in range(n_heads):
            xh = x[:, hd * V7X_LANES:(hd + 1) * V7X_LANES]
            r = (xh * cos + pltpu.roll(xh, V7X_LANES - 32, axis=1) * sin_lo
                 + pltpu.roll(xh, 32, axis=1) * sin_hi)
            if scale is not None:
                r = r * scale
            out_ref[:, hd * V7X_LANES:(hd + 1) * V7X_LANES] = r.astype(out_ref.dtype)

    rope_store(proj(0), q_ref, q_scale)
    rope_store(proj(1), k_ref, None)
    v_ref[...] = proj(2)

    cb = proj(3)
    u = proj(4) * proj(5)
    if tiles_per_seq > 1:
        @pl.when(i % tiles_per_seq == 0)
        def _():
            carry_ref[0:2, :] = st_ref[0]
    for s in range(n_seq):
        rows = slice(s * seq_rows, (s + 1) * seq_rows)
        us = u[rows]
        if tiles_per_seq > 1:
            hist0, hist1 = carry_ref[0:1, :], carry_ref[1:2, :]
        else:
            hist0, hist1 = st_ref[s, 0:1, :], st_ref[s, 1:2, :]
        row = lax.broadcasted_iota(jnp.int32, us.shape, 0)
        back1 = pltpu.roll(us, 1, axis=0)
        back2 = pltpu.roll(us, 2, axis=0)
        u1 = jnp.where(row == 0, hist1, back1)
        u2 = jnp.where(row == 0, hist0, jnp.where(row == 1, hist1, back2))
        conv = cw_ref[0:1, :] * u2 + cw_ref[1:2, :] * u1 + cw_ref[2:3, :] * us
        co_ref[rows, :] = (cb[rows] * conv).astype(co_ref.dtype)
        last = back2[0:2, :]
        sto_ref[s] = last
        if tiles_per_seq > 1:
            carry_ref[0:2, :] = last


def _inproj(x, norm_w, w_in, rope_tabs, conv_w, conv_state, seq_len, n_heads, q_scale):
    t, d = x.shape
    width = w_in.shape[1] // 6
    n_batch = conv_state.shape[0]
    hist = conv_state.shape[1]
    assert hist == 2 and seq_len >= hist and t == n_batch * seq_len
    tm = INPROJ_TOKEN_TILE
    if seq_len >= tm:
        assert seq_len % tm == 0
        n_seq, tiles_per_seq = 1, seq_len // tm
    else:
        assert tm % seq_len == 0 and seq_len % 8 == 0
        n_seq, tiles_per_seq = tm // seq_len, 1
    cos, sin_lo, sin_hi = rope_tabs
    pos_tiles = cos.shape[0] // tm
    tab_spec = pl.BlockSpec((tm, V7X_LANES), lambda i: (i % pos_tiles, 0))
    tok = lambda i: (i, 0)
    const = lambda i: (0, 0)
    st_spec = pl.BlockSpec((n_seq, hist, width), lambda i: (i // tiles_per_seq, 0, 0))
    vmem = _vmem_limit(
        pipelined=[((tm, d), F32), ((tm, width), BF16), ((tm, width), F32), ((tm, width), F32),
                   ((tm, width), BF16), ((tm, V7X_LANES), F32), ((tm, V7X_LANES), F32),
                   ((tm, V7X_LANES), F32)],
        resident=[(w_in.shape, BF16), ((tm, d), BF16)] + [((tm, width), F32)] * 8)
    return pl.pallas_call(
        functools.partial(_inproj_kernel, width=width, n_heads=n_heads, n_seq=n_seq,
                          tiles_per_seq=tiles_per_seq, q_scale=q_scale),
        name="inproj",
        grid=(t // tm,),
        in_specs=[
            pl.BlockSpec((tm, d), tok),
            pl.BlockSpec((1, d), const),
            pl.BlockSpec(w_in.shape, const, pipeline_mode=pl.Buffered(1)),
            tab_spec, tab_spec, tab_spec,
            pl.BlockSpec(conv_w.shape, const),
            st_spec,
        ],
        out_specs=[
            pl.BlockSpec((tm, width), tok),
            pl.BlockSpec((tm, width), tok),
            pl.BlockSpec((tm, width), tok),
            pl.BlockSpec((tm, width), tok),
            st_spec,
        ],
        out_shape=[
            jax.ShapeDtypeStruct((t, width), BF16),
            jax.ShapeDtypeStruct((t, width), F32),
            jax.ShapeDtypeStruct((t, width), F32),
            jax.ShapeDtypeStruct((t, width), BF16),
            jax.ShapeDtypeStruct(conv_state.shape, F32),
        ],
        scratch_shapes=[pltpu.VMEM((8, width), F32)],
        compiler_params=pltpu.CompilerParams(
            dimension_semantics=("arbitrary",), vmem_limit_bytes=vmem),
    )(x, norm_w, w_in, cos, sin_lo, sin_hi, conv_w, conv_state)


def _stack_components(q):
    lane = lax.broadcasted_iota(jnp.int32, q.shape, 1)
    half = q.shape[1] // 2
    zero = jnp.zeros_like(q)
    return jnp.concatenate([jnp.where(lane < half, q, zero), jnp.where(lane >= half, q, zero)], axis=0)


def _visible(shape, q_start, k_start):
    assert CHUNK & (CHUNK - 1) == 0
    rows = shape[1] // 2
    c = lax.broadcasted_iota(jnp.int32, shape, 1)
    q_pos = q_start + jnp.where(c >= rows, c - rows, c)
    k_pos = k_start + lax.broadcasted_iota(jnp.int32, shape, 0)
    return k_pos < (q_pos | (CHUNK - 1)) + 1


def _scores(k, q2):
    return lax.dot_general(k, q2, (((1,), (1,)), ((), ())), preferred_element_type=F32)


def _weighted_values(v, p):
    return lax.dot_general(v, p, (((0,), (0,)), ((), ())), preferred_element_type=F32)


def _lambda(lq1_ref, lk1_ref, lq2_ref, lk2_ref, lambda_init):
    s1 = jnp.sum(lq1_ref[...] * lk1_ref[...], axis=1, keepdims=True)
    s2 = jnp.sum(lq2_ref[...] * lk2_ref[...], axis=1, keepdims=True)
    return jnp.exp(s1) - jnp.exp(s2) + lambda_init


def _diff_finish(acc, l, lam, gain_col, lambda_init):
    rows = acc.shape[1] // 2
    o = acc[:, :rows] / l[:, :rows] - lam * (acc[:, rows:] / l[:, rows:])
    y = (o * lax.rsqrt(jnp.mean(o * o, axis=0, keepdims=True) + EPS)) * gain_col
    return (y * (1.0 - lambda_init)).T


def _attn_prompt_kernel(q_ref, qn_ref, k_ref, v_ref, lq1_ref, lk1_ref, lq2_ref, lk2_ref, g_ref, o_ref,
                        kb_ref, vt_ref, q2_ref, s_ref, cmax_ref, bias_ref, m_ref, acc_ref, *, lambda_init):
    i = pl.program_id(2)
    blk = q_ref.shape[0]
    e = v_ref.shape[1]
    n_blocks = kb_ref.shape[0]
    diag_slot = 2

    def scores(kb, slot, diagonal=False):
        s = _scores(kb_ref[kb], q2_ref[...])
        if diagonal:
            s = s + bias_ref[...]
        s_ref[slot] = s
        cmax_ref[slot] = jnp.max(s, axis=0, keepdims=True)

    def accumulate(kb, slot):
        m_prev = m_ref[...]
        m_new = jnp.maximum(m_prev, cmax_ref[slot])
        alpha = jnp.exp(m_prev - m_new)
        p = jnp.exp(s_ref[slot] - m_new).astype(BF16)
        acc_ref[...] = alpha * acc_ref[...] + jnp.dot(vt_ref[kb], p, preferred_element_type=F32)
        m_ref[...] = m_new

    m_ref[...] = jnp.full_like(m_ref, MASK_VALUE)
    acc_ref[...] = jnp.zeros_like(acc_ref)

    def finish():
        q2_ref[...] = _stack_components(qn_ref[...])
        scores(0, 0)
        accumulate(i, diag_slot)
        lam = _lambda(lq1_ref, lk1_ref, lq2_ref, lk2_ref, lambda_init)
        o_ref[...] = _diff_finish(acc_ref[0:e, :], acc_ref[e:e + 1, :], lam, g_ref[...],
                                  lambda_init).astype(o_ref.dtype)

    @pl.when((pl.program_id(0) == 0) & (pl.program_id(1) == 0) & (i == 0))
    def _():
        bias_ref[...] = jnp.where(_visible(bias_ref.shape, 0, 0), 0.0, MASK_VALUE)

    @pl.when(i == 0)
    def _():
        ones = jnp.ones((ONES_ROWS, blk), BF16)
        for j in range(n_blocks):
            rows = slice(j * blk, (j + 1) * blk)
            kb_ref[j] = k_ref[rows, :].astype(BF16)
            vt_ref[j] = jnp.concatenate([v_ref[rows, :].T.astype(BF16), ones], axis=0)
        q2_ref[...] = _stack_components(q_ref[...])
        scores(0, diag_slot, diagonal=True)
        finish()

    unroll = ATTN_LOOP_UNROLL

    def trip(j, carry):
        for u in range(unroll):
            scores(unroll * j + u + 1, (u + 1) % 2)
            accumulate(unroll * j + u, u % 2)
        return carry

    lax.fori_loop(0, jnp.maximum(i - 1, 0) // unroll, trip, 0)

    for rest in range(1, unroll + 1):
        @pl.when((i >= 1) & ((i - 1) % unroll == rest - 1))
        def _(rest=rest):
            first = i - rest
            for u in range(rest):
                if u + 1 < rest:
                    scores(first + u + 1, (u + 1) % 2)
                else:
                    scores(i, diag_slot, diagonal=True)
                accumulate(first + u, u % 2)
            finish()


def _attn_prompt(q, k, v, lams, gain_col, n_batch, seq_len, n_heads, lambda_init):
    t, width = q.shape
    e = width // n_heads
    blk = ATTN_BLOCK
    assert seq_len % blk == 0 and blk % CHUNK == 0 and e == V7X_LANES
    nq = seq_len // blk
    lam_spec = pl.BlockSpec(lams[0].shape, lambda b, h, i: (0, 0))
    n_blocks = seq_len // blk
    scratch = [((n_blocks, blk, e), BF16), ((n_blocks, e + ONES_ROWS, blk), BF16), ((2 * blk, e), BF16),
               ((3, blk, 2 * blk), F32), ((3, 1, 2 * blk), F32), ((blk, 2 * blk), F32), ((1, 2 * blk), F32),
               ((e + ONES_ROWS, 2 * blk), F32)]
    vmem = _vmem_limit(
        pipelined=[((blk, e), BF16)] * 3 + [((seq_len, e), F32)] * 2,
        resident=scratch + [((blk, 2 * blk), F32)] * 6)
    return pl.pallas_call(
        functools.partial(_attn_prompt_kernel, lambda_init=lambda_init),
        name="attn_prompt",
        grid=(n_batch, n_heads, nq),
        in_specs=[
            pl.BlockSpec((blk, e), lambda b, h, i: (b * nq + i, h)),
            pl.BlockSpec((blk, e), lambda b, h, i: (b * nq + jnp.minimum(i + 1, nq - 1), h)),
            pl.BlockSpec((seq_len, e), lambda b, h, i: (b, h)),
            pl.BlockSpec((seq_len, e), lambda b, h, i: (b, h)),
            lam_spec, lam_spec, lam_spec, lam_spec,
            pl.BlockSpec(gain_col.shape, lambda b, h, i: (0, 0)),
        ],
        out_specs=pl.BlockSpec((blk, e), lambda b, h, i: (b * nq + i, h)),
        out_shape=jax.ShapeDtypeStruct((t, width), BF16),
        scratch_shapes=[pltpu.VMEM(s, d) for s, d in scratch],
        compiler_params=pltpu.CompilerParams(
            dimension_semantics=("arbitrary", "arbitrary", "arbitrary"), vmem_limit_bytes=vmem),
    )(q, q, k, v, *lams, gain_col)


def _attn_sample_kernel(q_ref, kn_ref, vn_ref, kc_ref, vc_ref, lq1_ref, lk1_ref, lq2_ref, lk2_ref,
                        g_ref, o_ref, *, n_heads, past_len, lambda_init):
    rows = q_ref.shape[0]
    e = q_ref.shape[1] // n_heads
    lam = _lambda(lq1_ref, lk1_ref, lq2_ref, lk2_ref, lambda_init)
    gain_col = g_ref[...]
    visible = _visible((rows, 2 * rows), past_len, past_len)
    for hd in range(n_heads):
        cols = slice(hd * e, (hd + 1) * e)
        kc = kc_ref[pl.ds(hd, past_len, stride=n_heads), :].astype(BF16)
        vc = vc_ref[pl.ds(hd, past_len, stride=n_heads), :].astype(BF16)
        q2 = _stack_components(q_ref[:, cols])
        s_c = _scores(kc, q2)
        s_n = jnp.where(visible, _scores(kn_ref[:, cols].astype(BF16), q2), MASK_VALUE)
        m = jnp.maximum(jnp.max(s_c, axis=0, keepdims=True), jnp.max(s_n, axis=0, keepdims=True))
        p_c = jnp.exp(s_c - m)
        p_n = jnp.exp(s_n - m)
        l = jnp.sum(p_c, axis=0, keepdims=True) + jnp.sum(p_n, axis=0, keepdims=True)
        acc = (_weighted_values(vc, p_c.astype(BF16))
               + _weighted_values(vn_ref[:, cols].astype(BF16), p_n.astype(BF16)))
        o_ref[:, cols] = _diff_finish(acc, l, lam, gain_col, lambda_init).astype(o_ref.dtype)


def _attn_sample(q, k_new, v_new, k_cache, v_cache, lams, gain_col, n_batch, seq_len, n_heads,
                 lambda_init):
    t, width = q.shape
    e = width // n_heads
    past_len = k_cache.shape[0] // (n_batch * n_heads)
    assert seq_len % 8 == 0 and past_len % 8 == 0 and k_cache.shape[1] == e
    tok = lambda b: (b, 0)
    lam_spec = pl.BlockSpec(lams[0].shape, lambda b: (0, 0))
    vmem = _vmem_limit(
        pipelined=[((past_len * n_heads, e), F32)] * 2 + [((seq_len, width), F32)] * 4,
        resident=[((past_len, 2 * seq_len), F32)] * 4 + [((past_len, e), F32)] * 4)
    return pl.pallas_call(
        functools.partial(_attn_sample_kernel, n_heads=n_heads, past_len=past_len,
                          lambda_init=lambda_init),
        name="attn_sample",
        grid=(n_batch,),
        in_specs=[
            pl.BlockSpec((seq_len, width), tok),
            pl.BlockSpec((seq_len, width), tok),
            pl.BlockSpec((seq_len, width), tok),
            pl.BlockSpec((past_len * n_heads, e), tok),
            pl.BlockSpec((past_len * n_heads, e), tok),
            lam_spec, lam_spec, lam_spec, lam_spec,
            pl.BlockSpec(gain_col.shape, lambda b: (0, 0)),
        ],
        out_specs=pl.BlockSpec((seq_len, width), tok),
        out_shape=jax.ShapeDtypeStruct((t, width), BF16),
        compiler_params=pltpu.CompilerParams(
            dimension_semantics=("parallel",), vmem_limit_bytes=vmem),
    )(q, k_new, v_new, k_cache, v_cache, *lams, gain_col)


def _rope_tables(pos, head_dim, rows):
    half = head_dim // 2
    inv = ROPE_THETA ** (-jnp.arange(0, head_dim, 2, dtype=F32) / head_dim)
    ang = pos.astype(F32)[:, None] * inv[None, :]
    cos, sin = jnp.cos(ang), jnp.sin(ang)
    zero = jnp.zeros_like(sin)
    reps = V7X_LANES // head_dim
    cos_t = jnp.tile(jnp.concatenate([cos, cos], axis=1), (rows // pos.shape[0], reps))
    sin_lo = jnp.tile(jnp.concatenate([-sin, zero], axis=1), (rows // pos.shape[0], reps))
    sin_hi = jnp.tile(jnp.concatenate([zero, sin], axis=1), (rows // pos.shape[0], reps))
    return cos_t, sin_lo, sin_hi


def kernel(x_prompt, x_sample, cache_k, cache_v, state_conv, ffn1_norm, ffn1_w_gate, ffn1_w_up,
           ffn1_w_down, mix_norm, w_in, lambda_q1, lambda_k1, lambda_q2, lambda_k2, subln_gain,
           conv_w, w_out, ffn2_norm, ffn2_w_gate, ffn2_w_up, ffn2_w_down, final_norm):
    n_bp, seq_p, d = x_prompt.shape
    n_bs, seq_s, _ = x_sample.shape
    depth, _, past_len, n_heads, key_dim = cache_k.shape
    value_dim = cache_v.shape[-1]
    head_dim = key_dim // 2
    conv_ch = conv_w.shape[-1]
    attn_width = n_heads * key_dim
    assert key_dim == V7X_LANES and value_dim == V7X_LANES and attn_width == conv_ch
    q_scale = head_dim ** -0.5

    tabs_p = _rope_tables(jnp.arange(seq_p), head_dim, seq_p)
    tabs_s = _rope_tables(past_len + jnp.arange(seq_s), head_dim, max(seq_s, INPROJ_TOKEN_TILE))

    xp = x_prompt.reshape(n_bp * seq_p, d)
    xs = x_sample.reshape(n_bs * seq_s, d)
    row = lambda a: a.reshape(1, -1)
    outs = {name: [] for name in ("kp", "vp", "cp", "ks", "vs", "cs")}
    for l in range(depth):
        lambda_init = 0.8 - 0.6 * math.exp(-0.3 * l)
        last = l == depth - 1
        w1g, w1u, w1d = (_to_bf16(w[l]) for w in (ffn1_w_gate, ffn1_w_up, ffn1_w_down))
        w2g, w2u, w2d = (_to_bf16(w[l]) for w in (ffn2_w_gate, ffn2_w_up, ffn2_w_down))
        w_in_l, w_out_l = _to_bf16(w_in[l]), _to_bf16(w_out[l])
        lams = tuple(row(a[l]) for a in (lambda_q1, lambda_k1, lambda_q2, lambda_k2))
        gain = subln_gain[l].reshape(-1, 1)
        fin = row(final_norm)
        zeros_state = jnp.zeros((n_bp, conv_w.shape[1] - 1, conv_ch), F32)

        def layer(x, tabs, state, seq_len, attend):
            x = _ffn(x, row(ffn1_norm[l]), w1g, w1u, w1d, fin, False)
            q, k, v, conv_out, new_state = _inproj(
                x, row(mix_norm[l]), w_in_l, tabs, conv_w[l], state, seq_len, n_heads, q_scale)
            attn_out = attend(q, k, v)
            x = _ffn(x, row(ffn2_norm[l]), w2g, w2u, w2d, fin, last, mix=(attn_out, conv_out, w_out_l))
            return x, k, v, new_state

        xp, kp, vp, cp = layer(
            xp, tabs_p, zeros_state, seq_p,
            lambda q, k, v: _attn_prompt(q, k, v, lams, gain, n_bp, seq_p, n_heads, lambda_init))
        kc = cache_k[l].reshape(n_bs * past_len * n_heads, key_dim)
        vc = cache_v[l].reshape(n_bs * past_len * n_heads, value_dim)
        xs, ks, vs, cs = layer(
            xs, tabs_s, state_conv[l], seq_s,
            lambda q, k, v: _attn_sample(q, k, v, kc, vc, lams, gain, n_bs, seq_s, n_heads,
                                         lambda_init))
        outs["kp"].append(kp.reshape(n_bp, seq_p, n_heads, key_dim))
        outs["vp"].append(vp.reshape(n_bp, seq_p, n_heads, value_dim))
        outs["cp"].append(cp)
        outs["ks"].append(ks.reshape(n_bs, seq_s, n_heads, key_dim))
        outs["vs"].append(vs.reshape(n_bs, seq_s, n_heads, value_dim))
        outs["cs"].append(cs)
    return (xp.reshape(x_prompt.shape), xs.reshape(x_sample.shape),
            jnp.stack(outs["kp"]), jnp.stack(outs["vp"]), jnp.stack(outs["cp"]),
            jnp.stack(outs["ks"]), jnp.stack(outs["vs"]), jnp.stack(outs["cs"]))
```

```python
import functools
import math

import jax
import jax.numpy as jnp
from jax import lax
from jax.experimental import pallas as pl
from jax.experimental.pallas import tpu as pltpu

F32 = jnp.float32
BF16 = jnp.bfloat16

CHUNK = 64
ROPE_THETA = 10000.0
EPS = 1e-6
MASK_VALUE = -1e30

V7X_LANES = 128
V7X_VMEM_BYTES = 64 * 1024 * 1024
VMEM_BUDGET_BYTES = V7X_VMEM_BYTES * 7 // 8

CAST_BLOCK_BYTES = 6 * 1024 * 1024
FFN_TOKEN_TILE = 512
FFN_COL_TILE = 512
INPROJ_TOKEN_TILE = 256
OUTPROJ_TOKEN_TILE = 512
ATTN_BLOCK = 512
ATTN_LOOP_UNROLL = 4
ONES_ROWS = 16
assert CHUNK & (CHUNK - 1) == 0


def _nbytes(shape, dtype):
    return math.prod(shape) * jnp.dtype(dtype).itemsize


def _vmem_limit(pipelined, resident):
    need = 2 * sum(_nbytes(s, d) for s, d in pipelined) + sum(_nbytes(s, d) for s, d in resident)
    return int(min(VMEM_BUDGET_BYTES, max(need, 16 * 1024 * 1024)))


def _rms(x, gain):
    return (x * lax.rsqrt(jnp.mean(x * x, axis=-1, keepdims=True) + EPS)) * gain


def _cast_kernel(x_ref, o_ref):
    o_ref[...] = x_ref[...].astype(o_ref.dtype)


def _to_bf16(w):
    r, c = w.shape
    slab = r
    while slab % 32 == 0 and _nbytes((slab, c), w.dtype) > CAST_BLOCK_BYTES:
        slab //= 2
    return pl.pallas_call(
        _cast_kernel,
        name="cast",
        grid=(r // slab,),
        in_specs=[pl.BlockSpec((slab, c), lambda i: (i, 0))],
        out_specs=pl.BlockSpec((slab, c), lambda i: (i, 0)),
        out_shape=jax.ShapeDtypeStruct(w.shape, BF16),
        compiler_params=pltpu.CompilerParams(
            dimension_semantics=("parallel",), vmem_limit_bytes=VMEM_BUDGET_BYTES),
    )(w)


def _ffn_kernel(x_ref, nw_ref, wg0_ref, wg1_ref, wu0_ref, wu1_ref, wd0_ref, wd1_ref, fn_ref,
                o_ref, h_ref, *, n_chunks, apply_final_norm):
    f = pl.program_id(1)

    @pl.when(f == 0)
    def _():
        h_ref[...] = _rms(x_ref[...], nw_ref[...]).astype(BF16)
        o_ref[...] = jnp.zeros_like(o_ref)

    def chunk(wg_ref, wu_ref, wd_ref):
        h = h_ref[...]
        gate = jnp.dot(h, wg_ref[...], preferred_element_type=F32)
        up = jnp.dot(h, wu_ref[...], preferred_element_type=F32)
        act = ((gate * jax.nn.sigmoid(gate)) * up).astype(BF16)
        return jnp.dot(act, wd_ref[...], preferred_element_type=F32)

    @pl.when(2 * f + 1 < n_chunks)
    def _():
        o_ref[...] += chunk(wg0_ref, wu0_ref, wd0_ref) + chunk(wg1_ref, wu1_ref, wd1_ref)

    if n_chunks % 2:
        @pl.when(2 * f + 1 == n_chunks)
        def _():
            o_ref[...] += chunk(wg0_ref, wu0_ref, wd0_ref)

    @pl.when(f == pl.num_programs(1) - 1)
    def _():
        y = x_ref[...] + 0.5 * o_ref[...]
        if apply_final_norm:
            y = _rms(y, fn_ref[...])
        o_ref[...] = y


def _ffn(x, norm_w, w_gate, w_up, w_down, final_w, apply_final_norm):
    t, d = x.shape
    ffn = w_gate.shape[1]
    tm, tf = min(FFN_TOKEN_TILE, t), FFN_COL_TILE
    if t == 2 * FFN_TOKEN_TILE:
        tm, tf = t, FFN_COL_TILE // 2
    assert t % tm == 0 and ffn % tf == 0
    n_chunks = ffn // tf
    last = n_chunks - 1
    tok = lambda i, f: (i, 0)
    const = lambda i, f: (0, 0)
    col0 = pl.BlockSpec((d, tf), lambda i, f: (0, 2 * f))
    col1 = pl.BlockSpec((d, tf), lambda i, f: (0, jnp.minimum(2 * f + 1, last)))
    row0 = pl.BlockSpec((tf, d), lambda i, f: (2 * f, 0))
    row1 = pl.BlockSpec((tf, d), lambda i, f: (jnp.minimum(2 * f + 1, last), 0))
    vmem = _vmem_limit(
        pipelined=[((tm, d), F32), ((tm, d), F32)] + [((d, tf), BF16)] * 6,
        resident=[((tm, d), BF16), ((tm, d), F32)] + [((tm, tf), F32)] * 6)
    return pl.pallas_call(
        functools.partial(_ffn_kernel, n_chunks=n_chunks, apply_final_norm=apply_final_norm),
        name="ffn",
        grid=(t // tm, pl.cdiv(n_chunks, 2)),
        in_specs=[pl.BlockSpec((tm, d), tok), pl.BlockSpec((1, d), const),
                  col0, col1, col0, col1, row0, row1, pl.BlockSpec((1, d), const)],
        out_specs=pl.BlockSpec((tm, d), tok),
        out_shape=jax.ShapeDtypeStruct((t, d), F32),
        scratch_shapes=[pltpu.VMEM((tm, d), BF16)],
        compiler_params=pltpu.CompilerParams(
            dimension_semantics=("parallel", "arbitrary"), vmem_limit_bytes=vmem),
    )(x, norm_w, w_gate, w_gate, w_up, w_up, w_down, w_down, final_w)


def _outproj_kernel(x_ref, a_ref, c_ref, w_ref, o_ref):
    mixed = jnp.concatenate([a_ref[...], c_ref[...]], axis=1)
    o_ref[...] = x_ref[...] + jnp.dot(mixed, w_ref[...], preferred_element_type=F32)


def _outproj(x, attn_out, conv_out, w_out):
    t, d = x.shape
    wa, wc = attn_out.shape[1], conv_out.shape[1]
    tm = min(OUTPROJ_TOKEN_TILE, t)
    assert t % tm == 0 and wa + wc == w_out.shape[0]
    tok = lambda i: (i, 0)
    vmem = _vmem_limit(
        pipelined=[((tm, d), F32), ((tm, wa), BF16), ((tm, wc), BF16), (w_out.shape, BF16),
                   ((tm, d), F32)],
        resident=[((tm, wa + wc), BF16), ((tm, d), F32)])
    return pl.pallas_call(
        _outproj_kernel,
        name="outproj",
        grid=(t // tm,),
        in_specs=[
            pl.BlockSpec((tm, d), tok),
            pl.BlockSpec((tm, wa), tok),
            pl.BlockSpec((tm, wc), tok),
            pl.BlockSpec(w_out.shape, lambda i: (0, 0)),
        ],
        out_specs=pl.BlockSpec((tm, d), tok),
        out_shape=jax.ShapeDtypeStruct((t, d), F32),
        compiler_params=pltpu.CompilerParams(
            dimension_semantics=("parallel",), vmem_limit_bytes=vmem),
    )(x, attn_out, conv_out, w_out)


def _inproj_kernel(x_ref, nw_ref, w_ref, cos_ref, sin_lo_ref, sin_hi_ref, cw_ref, st_ref,
                   q_ref, k_ref, v_ref, co_ref, sto_ref, carry_ref,
                   *, width, n_heads, n_seq, tiles_per_seq, q_scale):
    i = pl.program_id(0)
    tm = x_ref.shape[0]
    seq_rows = tm // n_seq
    h = _rms(x_ref[...], nw_ref[...]).astype(BF16)

    def proj(col):
        return jnp.dot(h, w_ref[:, col * width:(col + 1) * width], preferred_element_type=F32)

    def rope_store(x, out_ref, scale):
        cos, sin_lo, sin_hi = cos_ref[...], sin_lo_ref[...], sin_hi_ref[...]
        for hd in range(n_heads):
            xh = x[:, hd * V7X_LANES:(hd + 1) * V7X_LANES]
            r = (xh * cos + pltpu.roll(xh, V7X_LANES - 32, axis=1) * sin_lo
                 + pltpu.roll(xh, 32, axis=1) * sin_hi)
            if scale is not None:
                r = r * scale
            out_ref[:, hd * V7X_LANES:(hd + 1) * V7X_LANES] = r.astype(out_ref.dtype)

    rope_store(proj(0), q_ref, q_scale)
    rope_store(proj(1), k_ref, None)
    v_ref[...] = proj(2)

    cb = proj(3)
    u = proj(4) * proj(5)
    if tiles_per_seq > 1:
        @pl.when(i % tiles_per_seq == 0)
        def _():
            carry_ref[0:2, :] = st_ref[0]
    for s in range(n_seq):
        rows = slice(s * seq_rows, (s + 1) * seq_rows)
        us = u[rows]
        if tiles_per_seq > 1:
            hist0, hist1 = carry_ref[0:1, :], carry_ref[1:2, :]
        else:
            hist0, hist1 = st_ref[s, 0:1, :], st_ref[s, 1:2, :]
        row = lax.broadcasted_iota(jnp.int32, us.shape, 0)
        back1 = pltpu.roll(us, 1, axis=0)
        back2 = pltpu.roll(us, 2, axis=0)
        u1 = jnp.where(row == 0, hist1, back1)
        u2 = jnp.where(row == 0, hist0, jnp.where(row == 1, hist1, back2))
        conv = cw_ref[0:1, :] * u2 + cw_ref[1:2, :] * u1 + cw_ref[2:3, :] * us
        co_ref[rows, :] = (cb[rows] * conv).astype(co_ref.dtype)
        last = back2[0:2, :]
        sto_ref[s] = last
        if tiles_per_seq > 1:
            carry_ref[0:2, :] = last


def _inproj(x, norm_w, w_in, rope_tabs, conv_w, conv_state, seq_len, n_heads, q_scale):
    t, d = x.shape
    width = w_in.shape[1] // 6
    n_batch = conv_state.shape[0]
    hist = conv_state.shape[1]
    assert hist == 2 and seq_len >= hist and t == n_batch * seq_len
    tm = INPROJ_TOKEN_TILE
    if seq_len >= tm:
        assert seq_len % tm == 0
        n_seq, tiles_per_seq = 1, seq_len // tm
    else:
        assert tm % seq_len == 0 and seq_len % 8 == 0
        n_seq, tiles_per_seq = tm // seq_len, 1
    cos, sin_lo, sin_hi = rope_tabs
    pos_tiles = cos.shape[0] // tm
    tab_spec = pl.BlockSpec((tm, V7X_LANES), lambda i: (i % pos_tiles, 0))
    tok = lambda i: (i, 0)
    const = lambda i: (0, 0)
    st_spec = pl.BlockSpec((n_seq, hist, width), lambda i: (i // tiles_per_seq, 0, 0))
    vmem = _vmem_limit(
        pipelined=[((tm, d), F32), ((tm, width), BF16), ((tm, width), F32), ((tm, width), F32),
                   ((tm, width), BF16), ((tm, V7X_LANES), F32), ((tm, V7X_LANES), F32),
                   ((tm, V7X_LANES), F32)],
        resident=[(w_in.shape, BF16), ((tm, d), BF16)] + [((tm, width), F32)] * 8)
    return pl.pallas_call(
        functools.partial(_inproj_kernel, width=width, n_heads=n_heads, n_seq=n_seq,
                          tiles_per_seq=tiles_per_seq, q_scale=q_scale),
        name="inproj",
        grid=(t // tm,),
        in_specs=[
            pl.BlockSpec((tm, d), tok),
            pl.BlockSpec((1, d), const),
            pl.BlockSpec(w_in.shape, const, pipeline_mode=pl.Buffered(1)),
            tab_spec, tab_spec, tab_spec,
            pl.BlockSpec(conv_w.shape, const),
            st_spec,
        ],
        out_specs=[
            pl.BlockSpec((tm, width), tok),
            pl.BlockSpec((tm, width), tok),
            pl.BlockSpec((tm, width), tok),
            pl.BlockSpec((tm, width), tok),
            st_spec,
        ],
        out_shape=[
            jax.ShapeDtypeStruct((t, width), BF16),
            jax.ShapeDtypeStruct((t, width), F32),
            jax.ShapeDtypeStruct((t, width), F32),
            jax.ShapeDtypeStruct((t, width), BF16),
            jax.ShapeDtypeStruct(conv_state.shape, F32),
        ],
        scratch_shapes=[pltpu.VMEM((8, width), F32)],
        compiler_params=pltpu.CompilerParams(
            dimension_semantics=("arbitrary",), vmem_limit_bytes=vmem),
    )(x, norm_w, w_in, cos, sin_lo, sin_hi, conv_w, conv_state)


def _stack_components(q):
    lane = lax.broadcasted_iota(jnp.int32, q.shape, 1)
    half = q.shape[1] // 2
    zero = jnp.zeros_like(q)
    return jnp.concatenate([jnp.where(lane < half, q, zero), jnp.where(lane >= half, q, zero)], axis=0)


def _visible(shape, q_start, k_start):
    assert CHUNK & (CHUNK - 1) == 0
    rows = shape[1] // 2
    c = lax.broadcasted_iota(jnp.int32, shape, 1)
    q_pos = q_start + jnp.where(c >= rows, c - rows, c)
    k_pos = k_start + lax.broadcasted_iota(jnp.int32, shape, 0)
    return k_pos < (q_pos | (CHUNK - 1)) + 1


def _scores(k, q2):
    return lax.dot_general(k, q2, (((1,), (1,)), ((), ())), preferred_element_type=F32)


def _weighted_values(v, p):
    return lax.dot_general(v, p, (((0,), (0,)), ((), ())), preferred_element_type=F32)


def _lambda(lq1_ref, lk1_ref, lq2_ref, lk2_ref, lambda_init):
    s1 = jnp.sum(lq1_ref[...] * lk1_ref[...], axis=1, keepdims=True)
    s2 = jnp.sum(lq2_ref[...] * lk2_ref[...], axis=1, keepdims=True)
    return jnp.exp(s1) - jnp.exp(s2) + lambda_init


def _diff_finish(acc, l, lam, gain_col, lambda_init):
    rows = acc.shape[1] // 2
    o = acc[:, :rows] / l[:, :rows] - lam * (acc[:, rows:] / l[:, rows:])
    y = (o * lax.rsqrt(jnp.mean(o * o, axis=0, keepdims=True) + EPS)) * gain_col
    return (y * (1.0 - lambda_init)).T


def _attn_prompt_kernel(q_ref, qn_ref, k_ref, v_ref, lq1_ref, lk1_ref, lq2_ref, lk2_ref, g_ref, o_ref,
                        kb_ref, vt_ref, q2_ref, s_ref, cmax_ref, bias_ref, m_ref, acc_ref, *, lambda_init):
    i = pl.program_id(2)
    blk = q_ref.shape[0]
    e = v_ref.shape[1]
    n_blocks = kb_ref.shape[0]
    diag_slot = 2

    def scores(kb, slot, diagonal=False):
        s = _scores(kb_ref[kb], q2_ref[...])
        if diagonal:
            s = s + bias_ref[...]
        s_ref[slot] = s
        cmax_ref[slot] = jnp.max(s, axis=0, keepdims=True)

    def accumulate(kb, slot):
        m_prev = m_ref[...]
        m_new = jnp.maximum(m_prev, cmax_ref[slot])
        alpha = jnp.exp(m_prev - m_new)
        p = jnp.exp(s_ref[slot] - m_new).astype(BF16)
        acc_ref[...] = alpha * acc_ref[...] + jnp.dot(vt_ref[kb], p, preferred_element_type=F32)
        m_ref[...] = m_new

    m_ref[...] = jnp.full_like(m_ref, MASK_VALUE)
    acc_ref[...] = jnp.zeros_like(acc_ref)

    def finish():
        q2_ref[...] = _stack_components(qn_ref[...])
        scores(0, 0)
        accumulate(i, diag_slot)
        lam = _lambda(lq1_ref, lk1_ref, lq2_ref, lk2_ref, lambda_init)
        o_ref[...] = _diff_finish(acc_ref[0:e, :], acc_ref[e:e + 1, :], lam, g_ref[...],
                                  lambda_init).astype(o_ref.dtype)

    @pl.when((pl.program_id(0) == 0) & (pl.program_id(1) == 0) & (i == 0))
    def _():
        bias_ref[...] = jnp.where(_visible(bias_ref.shape, 0, 0), 0.0, MASK_VALUE)

    @pl.when(i == 0)
    def _():
        ones = jnp.ones((ONES_ROWS, blk), BF16)
        for j in range(n_blocks):
            rows = slice(j * blk, (j + 1) * blk)
            kb_ref[j] = k_ref[rows, :].astype(BF16)
            vt_ref[j] = jnp.concatenate([v_ref[rows, :].T.astype(BF16), ones], axis=0)
        q2_ref[...] = _stack_components(q_ref[...])
        scores(0, diag_slot, diagonal=True)
        finish()

    unroll = ATTN_LOOP_UNROLL

    def trip(j, carry):
        for u in range(unroll):
            scores(unroll * j + u + 1, (u + 1) % 2)
            accumulate(unroll * j + u, u % 2)
        return carry

    lax.fori_loop(0, jnp.maximum(i - 1, 0) // unroll, trip, 0)

    for rest in range(1, unroll + 1):
        @pl.when((i >= 1) & ((i - 1) % unroll == rest - 1))
        def _(rest=rest):
            first = i - rest
            for u in range(rest):
                if u + 1 < rest:
                    scores(first + u + 1, (u + 1) % 2)
                else:
                    scores(i, diag_slot, diagonal=True)
                accumulate(first + u, u % 2)
            finish()


def _attn_prompt(q, k, v, lams, gain_col, n_batch, seq_len, n_heads, lambda_init):
    t, width = q.shape
    e = width // n_heads
    blk = ATTN_BLOCK
    assert seq_len % blk == 0 and blk % CHUNK == 0 and e == V7X_LANES
    nq = seq_len // blk
    lam_spec = pl.BlockSpec(lams[0].shape, lambda b, h, i: (0, 0))
    n_blocks = seq_len // blk
    scratch = [((n_blocks, blk, e), BF16), ((n_blocks, e + ONES_ROWS, blk), BF16), ((2 * blk, e), BF16),
               ((3, blk, 2 * blk), F32), ((3, 1, 2 * blk), F32), ((blk, 2 * blk), F32), ((1, 2 * blk), F32),
               ((e + ONES_ROWS, 2 * blk), F32)]
    vmem = _vmem_limit(
        pipelined=[((blk, e), BF16)] * 3 + [((seq_len, e), F32)] * 2,
        resident=scratch + [((blk, 2 * blk), F32)] * 6)
    return pl.pallas_call(
        functools.partial(_attn_prompt_kernel, lambda_init=lambda_init),
        name="attn_prompt",
        grid=(n_batch, n_heads, nq),
        in_specs=[
            pl.BlockSpec((blk, e), lambda b, h, i: (b * nq + i, h)),
            pl.BlockSpec((blk, e), lambda b, h, i: (b * nq + jnp.minimum(i + 1, nq - 1), h)),
            pl.BlockSpec((seq_len, e), lambda b, h, i: (b, h)),
            pl.BlockSpec((seq_len, e), lambda b, h, i: (b, h)),
            lam_spec, lam_spec, lam_spec, lam_spec,
            pl.BlockSpec(gain_col.shape, lambda b, h, i: (0, 0)),
        ],
        out_specs=pl.BlockSpec((blk, e), lambda b, h, i: (b * nq + i, h)),
        out_shape=jax.ShapeDtypeStruct((t, width), BF16),
        scratch_shapes=[pltpu.VMEM(s, d) for s, d in scratch],
        compiler_params=pltpu.CompilerParams(
            dimension_semantics=("arbitrary", "arbitrary", "arbitrary"), vmem_limit_bytes=vmem),
    )(q, q, k, v, *lams, gain_col)


def _attn_sample_kernel(q_ref, kn_ref, vn_ref, kc_ref, vc_ref, lq1_ref, lk1_ref, lq2_ref, lk2_ref,
                        g_ref, o_ref, *, n_heads, past_len, lambda_init):
    rows = q_ref.shape[0]
    e = q_ref.shape[1] // n_heads
    lam = _lambda(lq1_ref, lk1_ref, lq2_ref, lk2_ref, lambda_init)
    gain_col = g_ref[...]
    visible = _visible((rows, 2 * rows), past_len, past_len)
    for hd in range(n_heads):
        cols = slice(hd * e, (hd + 1) * e)
        kc = kc_ref[pl.ds(hd, past_len, stride=n_heads), :].astype(BF16)
        vc = vc_ref[pl.ds(hd, past_len, stride=n_heads), :].astype(BF16)
        q2 = _stack_components(q_ref[:, cols])
        s_c = _scores(kc, q2)
        s_n = jnp.where(visible, _scores(kn_ref[:, cols].astype(BF16), q2), MASK_VALUE)
        m = jnp.maximum(jnp.max(s_c, axis=0, keepdims=True), jnp.max(s_n, axis=0, keepdims=True))
        p_c = jnp.exp(s_c - m)
        p_n = jnp.exp(s_n - m)
        l = jnp.sum(p_c, axis=0, keepdims=True) + jnp.sum(p_n, axis=0, keepdims=True)
        acc = (_weighted_values(vc, p_c.astype(BF16))
               + _weighted_values(vn_ref[:, cols].astype(BF16), p_n.astype(BF16)))
        o_ref[:, cols] = _diff_finish(acc, l, lam, gain_col, lambda_init).astype(o_ref.dtype)


def _attn_sample(q, k_new, v_new, k_cache, v_cache, lams, gain_col, n_batch, seq_len, n_heads,
                 lambda_init):
    t, width = q.shape
    e = width // n_heads
    past_len = k_cache.shape[0] // (n_batch * n_heads)
    assert seq_len % 8 == 0 and past_len % 8 == 0 and k_cache.shape[1] == e
    tok = lambda b: (b, 0)
    lam_spec = pl.BlockSpec(lams[0].shape, lambda b: (0, 0))
    vmem = _vmem_limit(
        pipelined=[((past_len * n_heads, e), F32)] * 2 + [((seq_len, width), F32)] * 4,
        resident=[((past_len, 2 * seq_len), F32)] * 4 + [((past_len, e), F32)] * 4)
    return pl.pallas_call(
        functools.partial(_attn_sample_kernel, n_heads=n_heads, past_len=past_len,
                          lambda_init=lambda_init),
        name="attn_sample",
        grid=(n_batch,),
        in_specs=[
            pl.BlockSpec((seq_len, width), tok),
            pl.BlockSpec((seq_len, width), tok),
            pl.BlockSpec((seq_len, width), tok),
            pl.BlockSpec((past_len * n_heads, e), tok),
            pl.BlockSpec((past_len * n_heads, e), tok),
            lam_spec, lam_spec, lam_spec, lam_spec,
            pl.BlockSpec(gain_col.shape, lambda b: (0, 0)),
        ],
        out_specs=pl.BlockSpec((seq_len, width), tok),
        out_shape=jax.ShapeDtypeStruct((t, width), BF16),
        compiler_params=pltpu.CompilerParams(
            dimension_semantics=("parallel",), vmem_limit_bytes=vmem),
    )(q, k_new, v_new, k_cache, v_cache, *lams, gain_col)


def _rope_tables(pos, head_dim, rows):
    half = head_dim // 2
    inv = ROPE_THETA ** (-jnp.arange(0, head_dim, 2, dtype=F32) / head_dim)
    ang = pos.astype(F32)[:, None] * inv[None, :]
    cos, sin = jnp.cos(ang), jnp.sin(ang)
    zero = jnp.zeros_like(sin)
    reps = V7X_LANES // head_dim
    cos_t = jnp.tile(jnp.concatenate([cos, cos], axis=1), (rows // pos.shape[0], reps))
    sin_lo = jnp.tile(jnp.concatenate([-sin, zero], axis=1), (rows // pos.shape[0], reps))
    sin_hi = jnp.tile(jnp.concatenate([zero, sin], axis=1), (rows // pos.shape[0], reps))
    return cos_t, sin_lo, sin_hi


def kernel(x_prompt, x_sample, cache_k, cache_v, state_conv, ffn1_norm, ffn1_w_gate, ffn1_w_up,
           ffn1_w_down, mix_norm, w_in, lambda_q1, lambda_k1, lambda_q2, lambda_k2, subln_gain,
           conv_w, w_out, ffn2_norm, ffn2_w_gate, ffn2_w_up, ffn2_w_down, final_norm):
    n_bp, seq_p, d = x_prompt.shape
    n_bs, seq_s, _ = x_sample.shape
    depth, _, past_len, n_heads, key_dim = cache_k.shape
    value_dim = cache_v.shape[-1]
    head_dim = key_dim // 2
    conv_ch = conv_w.shape[-1]
    attn_width = n_heads * key_dim
    assert key_dim == V7X_LANES and value_dim == V7X_LANES and attn_width == conv_ch
    q_scale = head_dim ** -0.5

    tabs_p = _rope_tables(jnp.arange(seq_p), head_dim, seq_p)
    tabs_s = _rope_tables(past_len + jnp.arange(seq_s), head_dim, max(seq_s, INPROJ_TOKEN_TILE))

    xp = x_prompt.reshape(n_bp * seq_p, d)
    xs = x_sample.reshape(n_bs * seq_s, d)
    row = lambda a: a.reshape(1, -1)
    outs = {name: [] for name in ("kp", "vp", "cp", "ks", "vs", "cs")}
    for l in range(depth):
        lambda_init = 0.8 - 0.6 * math.exp(-0.3 * l)
        last = l == depth - 1
        w1g, w1u, w1d = (_to_bf16(w[l]) for w in (ffn1_w_gate, ffn1_w_up, ffn1_w_down))
        w2g, w2u, w2d = (_to_bf16(w[l]) for w in (ffn2_w_gate, ffn2_w_up, ffn2_w_down))
        w_in_l, w_out_l = _to_bf16(w_in[l]), _to_bf16(w_out[l])
        lams = tuple(row(a[l]) for a in (lambda_q1, lambda_k1, lambda_q2, lambda_k2))
        gain = subln_gain[l].reshape(-1, 1)
        fin = row(final_norm)
        zeros_state = jnp.zeros((n_bp, conv_w.shape[1] - 1, conv_ch), F32)

        def layer(x, tabs, state, seq_len, attend):
            x = _ffn(x, row(ffn1_norm[l]), w1g, w1u, w1d, fin, False)
            q, k, v, conv_out, new_state = _inproj(
                x, row(mix_norm[l]), w_in_l, tabs, conv_w[l], state, seq_len, n_heads, q_scale)
            attn_out = attend(q, k, v)
            x = _outproj(x, attn_out, conv_out, w_out_l)
            x = _ffn(x, row(ffn2_norm[l]), w2g, w2u, w2d, fin, last)
            return x, k, v, new_state

        xp, kp, vp, cp = layer(
            xp, tabs_p, zeros_state, seq_p,
            lambda q, k, v: _attn_prompt(q, k, v, lams, gain, n_bp, seq_p, n_heads, lambda_init))
        kc = cache_k[l].reshape(n_bs * past_len * n_heads, key_dim)
        vc = cache_v[l].reshape(n_bs * past_len * n_heads, value_dim)
        xs, ks, vs, cs = layer(
            xs, tabs_s, state_conv[l], seq_s,
            lambda q, k, v: _attn_sample(q, k, v, kc, vc, lams, gain, n_bs, seq_s, n_heads,
                                         lambda_init))
        outs["kp"].append(kp.reshape(n_bp, seq_p, n_heads, key_dim))
        outs["vp"].append(vp.reshape(n_bp, seq_p, n_heads, value_dim))
        outs["cp"].append(cp)
        outs["ks"].append(ks.reshape(n_bs, seq_s, n_heads, key_dim))
        outs["vs"].append(vs.reshape(n_bs, seq_s, n_heads, value_dim))
        outs["cs"].append(cs)
    return (xp.reshape(x_prompt.shape), xs.reshape(x_sample.shape),
            jnp.stack(outs["kp"]), jnp.stack(outs["vp"]), jnp.stack(outs["cp"]),
            jnp.stack(outs["ks"]), jnp.stack(outs["vs"]), jnp.stack(outs["cs"]))
```

```python
import functools
import math

import jax
import jax.numpy as jnp
from jax import lax
from jax.experimental import pallas as pl
from jax.experimental.pallas import tpu as pltpu

F32 = jnp.float32
BF16 = jnp.bfloat16

CHUNK = 64
ROPE_THETA = 10000.0
EPS = 1e-6
MASK_VALUE = -1e30

V7X_LANES = 128
V7X_VMEM_BYTES = 64 * 1024 * 1024
VMEM_BUDGET_BYTES = V7X_VMEM_BYTES * 7 // 8

CAST_BLOCK_BYTES = 6 * 1024 * 1024
FFN_TOKEN_TILE = 512
FFN_COL_TILE = 512
INPROJ_TOKEN_TILE = 256
OUTPROJ_TOKEN_TILE = 512
ATTN_BLOCK = 512
ATTN_LOOP_UNROLL = 4
ONES_ROWS = 16
assert CHUNK & (CHUNK - 1) == 0


def _nbytes(shape, dtype):
    return math.prod(shape) * jnp.dtype(dtype).itemsize


def _vmem_limit(pipelined, resident):
    need = 2 * sum(_nbytes(s, d) for s, d in pipelined) + sum(_nbytes(s, d) for s, d in resident)
    return int(min(VMEM_BUDGET_BYTES, max(need, 16 * 1024 * 1024)))


def _rms(x, gain):
    return (x * lax.rsqrt(jnp.mean(x * x, axis=-1, keepdims=True) + EPS)) * gain


def _cast_kernel(x_ref, o_ref):
    o_ref[...] = x_ref[...].astype(o_ref.dtype)


def _to_bf16(w):
    r, c = w.shape
    slab = r
    while slab % 32 == 0 and _nbytes((slab, c), w.dtype) > CAST_BLOCK_BYTES:
        slab //= 2
    return pl.pallas_call(
        _cast_kernel,
        name="cast",
        grid=(r // slab,),
        in_specs=[pl.BlockSpec((slab, c), lambda i: (i, 0))],
        out_specs=pl.BlockSpec((slab, c), lambda i: (i, 0)),
        out_shape=jax.ShapeDtypeStruct(w.shape, BF16),
        compiler_params=pltpu.CompilerParams(
            dimension_semantics=("parallel",), vmem_limit_bytes=VMEM_BUDGET_BYTES),
    )(w)


def _ffn_kernel(x_ref, nw_ref, wg0_ref, wg1_ref, wu0_ref, wu1_ref, wd0_ref, wd1_ref, fn_ref,
                o_ref, h_ref, *, n_chunks, apply_final_norm):
    f = pl.program_id(1)

    @pl.when(f == 0)
    def _():
        h_ref[...] = _rms(x_ref[...], nw_ref[...]).astype(BF16)
        o_ref[...] = jnp.zeros_like(o_ref)

    def chunk(wg_ref, wu_ref, wd_ref):
        h = h_ref[...]
        gate = jnp.dot(h, wg_ref[...], preferred_element_type=F32)
        up = jnp.dot(h, wu_ref[...], preferred_element_type=F32)
        act = ((gate * jax.nn.sigmoid(gate)) * up).astype(BF16)
        return jnp.dot(act, wd_ref[...], preferred_element_type=F32)

    def pair():
        o_ref[...] += chunk(wg0_ref, wu0_ref, wd0_ref) + chunk(wg1_ref, wu1_ref, wd1_ref)

    if n_chunks % 2:
        @pl.when(f == 0)
        def _():
            o_ref[...] += chunk(wg0_ref, wu0_ref, wd0_ref)

        pl.when(f > 0)(pair)
    else:
        pair()

    @pl.when(f == pl.num_programs(1) - 1)
    def _():
        y = x_ref[...] + 0.5 * o_ref[...]
        if apply_final_norm:
            y = _rms(y, fn_ref[...])
        o_ref[...] = y


def _ffn(x, norm_w, w_gate, w_up, w_down, final_w, apply_final_norm):
    t, d = x.shape
    ffn = w_gate.shape[1]
    tm, tf = min(FFN_TOKEN_TILE, t), FFN_COL_TILE
    if t == 2 * FFN_TOKEN_TILE:
        tm, tf = t, FFN_COL_TILE // 2
    assert t % tm == 0 and ffn % tf == 0
    n_chunks = ffn // tf
    assert n_chunks >= 2
    if n_chunks % 2:
        first = lambda f: jnp.maximum(2 * f - 1, 0)
        second = lambda f: jnp.maximum(2 * f, 2)
    else:
        first = lambda f: 2 * f
        second = lambda f: 2 * f + 1
    tok = lambda i, f: (i, 0)
    const = lambda i, f: (0, 0)
    col0 = pl.BlockSpec((d, tf), lambda i, f: (0, first(f)))
    col1 = pl.BlockSpec((d, tf), lambda i, f: (0, second(f)))
    row0 = pl.BlockSpec((tf, d), lambda i, f: (first(f), 0))
    row1 = pl.BlockSpec((tf, d), lambda i, f: (second(f), 0))
    vmem = _vmem_limit(
        pipelined=[((tm, d), F32), ((tm, d), F32)] + [((d, tf), BF16)] * 6,
        resident=[((tm, d), BF16), ((tm, d), F32)] + [((tm, tf), F32)] * 6)
    return pl.pallas_call(
        functools.partial(_ffn_kernel, n_chunks=n_chunks, apply_final_norm=apply_final_norm),
        name="ffn",
        grid=(t // tm, pl.cdiv(n_chunks, 2)),
        in_specs=[pl.BlockSpec((tm, d), tok), pl.BlockSpec((1, d), const),
                  col0, col1, col0, col1, row0, row1, pl.BlockSpec((1, d), const)],
        out_specs=pl.BlockSpec((tm, d), tok),
        out_shape=jax.ShapeDtypeStruct((t, d), F32),
        scratch_shapes=[pltpu.VMEM((tm, d), BF16)],
        compiler_params=pltpu.CompilerParams(
            dimension_semantics=("parallel", "arbitrary"), vmem_limit_bytes=vmem),
    )(x, norm_w, w_gate, w_gate, w_up, w_up, w_down, w_down, final_w)


def _outproj_kernel(x_ref, a_ref, c_ref, w_ref, o_ref):
    mixed = jnp.concatenate([a_ref[...], c_ref[...]], axis=1)
    o_ref[...] = x_ref[...] + jnp.dot(mixed, w_ref[...], preferred_element_type=F32)


def _outproj(x, attn_out, conv_out, w_out):
    t, d = x.shape
    wa, wc = attn_out.shape[1], conv_out.shape[1]
    tm = min(OUTPROJ_TOKEN_TILE, t)
    assert t % tm == 0 and wa + wc == w_out.shape[0]
    tok = lambda i: (i, 0)
    vmem = _vmem_limit(
        pipelined=[((tm, d), F32), ((tm, wa), BF16), ((tm, wc), BF16), (w_out.shape, BF16),
                   ((tm, d), F32)],
        resident=[((tm, wa + wc), BF16), ((tm, d), F32)])
    return pl.pallas_call(
        _outproj_kernel,
        name="outproj",
        grid=(t // tm,),
        in_specs=[
            pl.BlockSpec((tm, d), tok),
            pl.BlockSpec((tm, wa), tok),
            pl.BlockSpec((tm, wc), tok),
            pl.BlockSpec(w_out.shape, lambda i: (0, 0)),
        ],
        out_specs=pl.BlockSpec((tm, d), tok),
        out_shape=jax.ShapeDtypeStruct((t, d), F32),
        compiler_params=pltpu.CompilerParams(
            dimension_semantics=("parallel",), vmem_limit_bytes=vmem),
    )(x, attn_out, conv_out, w_out)


def _inproj_kernel(x_ref, nw_ref, w_ref, cos_ref, sin_lo_ref, sin_hi_ref, cw_ref, st_ref,
                   q_ref, k_ref, v_ref, co_ref, sto_ref, carry_ref,
                   *, width, n_heads, n_seq, tiles_per_seq, q_scale):
    i = pl.program_id(0)
    tm = x_ref.shape[0]
    seq_rows = tm // n_seq
    h = _rms(x_ref[...], nw_ref[...]).astype(BF16)

    def proj(col):
        return jnp.dot(h, w_ref[:, col * width:(col + 1) * width], preferred_element_type=F32)

    def rope_store(x, out_ref, scale):
        cos, sin_lo, sin_hi = cos_ref[...], sin_lo_ref[...], sin_hi_ref[...]
        for hd in range(n_heads):
            xh = x[:, hd * V7X_LANES:(hd + 1) * V7X_LANES]
            r = (xh * cos + pltpu.roll(xh, V7X_LANES - 32, axis=1) * sin_lo
                 + pltpu.roll(xh, 32, axis=1) * sin_hi)
            if scale is not None:
                r = r * scale
            out_ref[:, hd * V7X_LANES:(hd + 1) * V7X_LANES] = r.astype(out_ref.dtype)

    rope_store(proj(0), q_ref, q_scale)
    rope_store(proj(1), k_ref, None)
    v_ref[...] = proj(2)

    cb = proj(3)
    u = proj(4) * proj(5)
    if tiles_per_seq > 1:
        @pl.when(i % tiles_per_seq == 0)
        def _():
            carry_ref[0:2, :] = st_ref[0]
    for s in range(n_seq):
        rows = slice(s * seq_rows, (s + 1) * seq_rows)
        us = u[rows]
        if tiles_per_seq > 1:
            hist0, hist1 = carry_ref[0:1, :], carry_ref[1:2, :]
        else:
            hist0, hist1 = st_ref[s, 0:1, :], st_ref[s, 1:2, :]
        row = lax.broadcasted_iota(jnp.int32, us.shape, 0)
        back1 = pltpu.roll(us, 1, axis=0)
        back2 = pltpu.roll(us, 2, axis=0)
        u1 = jnp.where(row == 0, hist1, back1)
        u2 = jnp.where(row == 0, hist0, jnp.where(row == 1, hist1, back2))
        conv = cw_ref[0:1, :] * u2 + cw_ref[1:2, :] * u1 + cw_ref[2:3, :] * us
        co_ref[rows, :] = (cb[rows] * conv).astype(co_ref.dtype)
        last = back2[0:2, :]
        sto_ref[s] = last
        if tiles_per_seq > 1:
            carry_ref[0:2, :] = last


def _inproj(x, norm_w, w_in, rope_tabs, conv_w, conv_state, seq_len, n_heads, q_scale):
    t, d = x.shape
    width = w_in.shape[1] // 6
    n_batch = conv_state.shape[0]
    hist = conv_state.shape[1]
    assert hist == 2 and seq_len >= hist and t == n_batch * seq_len
    tm = INPROJ_TOKEN_TILE
    if seq_len >= tm:
        assert seq_len % tm == 0
        n_seq, tiles_per_seq = 1, seq_len // tm
    else:
        assert tm % seq_len == 0 and seq_len % 8 == 0
        n_seq, tiles_per_seq = tm // seq_len, 1
    cos, sin_lo, sin_hi = rope_tabs
    pos_tiles = cos.shape[0] // tm
    tab_spec = pl.BlockSpec((tm, V7X_LANES), lambda i: (i % pos_tiles, 0))
    tok = lambda i: (i, 0)
    const = lambda i: (0, 0)
    st_spec = pl.BlockSpec((n_seq, hist, width), lambda i: (i // tiles_per_seq, 0, 0))
    vmem = _vmem_limit(
        pipelined=[((tm, d), F32), ((tm, width), BF16), ((tm, width), F32), ((tm, width), F32),
                   ((tm, width), BF16), ((tm, V7X_LANES), F32), ((tm, V7X_LANES), F32),
                   ((tm, V7X_LANES), F32)],
        resident=[(w_in.shape, BF16), ((tm, d), BF16)] + [((tm, width), F32)] * 8)
    return pl.pallas_call(
        functools.partial(_inproj_kernel, width=width, n_heads=n_heads, n_seq=n_seq,
                          tiles_per_seq=tiles_per_seq, q_scale=q_scale),
        name="inproj",
        grid=(t // tm,),
        in_specs=[
            pl.BlockSpec((tm, d), tok),
            pl.BlockSpec((1, d), const),
            pl.BlockSpec(w_in.shape, const, pipeline_mode=pl.Buffered(1)),
            tab_spec, tab_spec, tab_spec,
            pl.BlockSpec(conv_w.shape, const),
            st_spec,
        ],
        out_specs=[
            pl.BlockSpec((tm, width), tok),
            pl.BlockSpec((tm, width), tok),
            pl.BlockSpec((tm, width), tok),
            pl.BlockSpec((tm, width), tok),
            st_spec,
        ],
        out_shape=[
            jax.ShapeDtypeStruct((t, width), BF16),
            jax.ShapeDtypeStruct((t, width), F32),
            jax.ShapeDtypeStruct((t, width), F32),
            jax.ShapeDtypeStruct((t, width), BF16),
            jax.ShapeDtypeStruct(conv_state.shape, F32),
        ],
        scratch_shapes=[pltpu.VMEM((8, width), F32)],
        compiler_params=pltpu.CompilerParams(
            dimension_semantics=("arbitrary",), vmem_limit_bytes=vmem),
    )(x, norm_w, w_in, cos, sin_lo, sin_hi, conv_w, conv_state)


def _stack_components(q):
    lane = lax.broadcasted_iota(jnp.int32, q.shape, 1)
    half = q.shape[1] // 2
    zero = jnp.zeros_like(q)
    return jnp.concatenate([jnp.where(lane < half, q, zero), jnp.where(lane >= half, q, zero)], axis=0)


def _visible(shape, q_start, k_start):
    assert CHUNK & (CHUNK - 1) == 0
    rows = shape[1] // 2
    c = lax.broadcasted_iota(jnp.int32, shape, 1)
    q_pos = q_start + jnp.where(c >= rows, c - rows, c)
    k_pos = k_start + lax.broadcasted_iota(jnp.int32, shape, 0)
    return k_pos < (q_pos | (CHUNK - 1)) + 1


def _scores(k, q2):
    return lax.dot_general(k, q2, (((1,), (1,)), ((), ())), preferred_element_type=F32)


def _weighted_values(v, p):
    return lax.dot_general(v, p, (((0,), (0,)), ((), ())), preferred_element_type=F32)


def _lambda(lq1_ref, lk1_ref, lq2_ref, lk2_ref, lambda_init):
    s1 = jnp.sum(lq1_ref[...] * lk1_ref[...], axis=1, keepdims=True)
    s2 = jnp.sum(lq2_ref[...] * lk2_ref[...], axis=1, keepdims=True)
    return jnp.exp(s1) - jnp.exp(s2) + lambda_init


def _diff_finish(acc, l, lam, gain_col, lambda_init):
    rows = acc.shape[1] // 2
    o = acc[:, :rows] / l[:, :rows] - lam * (acc[:, rows:] / l[:, rows:])
    y = (o * lax.rsqrt(jnp.mean(o * o, axis=0, keepdims=True) + EPS)) * gain_col
    return (y * (1.0 - lambda_init)).T


def _attn_prompt_kernel(q_ref, qn_ref, k_ref, v_ref, lq1_ref, lk1_ref, lq2_ref, lk2_ref, g_ref, o_ref,
                        kb_ref, vt_ref, q2_ref, s_ref, cmax_ref, bias_ref, m_ref, acc_ref, *, lambda_init):
    i = pl.program_id(2)
    blk = q_ref.shape[0]
    e = v_ref.shape[1]
    n_blocks = kb_ref.shape[0]
    diag_slot = 2

    def scores(kb, slot, diagonal=False):
        s = _scores(kb_ref[kb], q2_ref[...])
        if diagonal:
            s = s + bias_ref[...]
        s_ref[slot] = s
        cmax_ref[slot] = jnp.max(s, axis=0, keepdims=True)

    def accumulate(kb, slot):
        m_prev = m_ref[...]
        m_new = jnp.maximum(m_prev, cmax_ref[slot])
        alpha = jnp.exp(m_prev - m_new)
        p = jnp.exp(s_ref[slot] - m_new).astype(BF16)
        acc_ref[...] = alpha * acc_ref[...] + jnp.dot(vt_ref[kb], p, preferred_element_type=F32)
        m_ref[...] = m_new

    m_ref[...] = jnp.full_like(m_ref, MASK_VALUE)
    acc_ref[...] = jnp.zeros_like(acc_ref)

    def finish():
        q2_ref[...] = _stack_components(qn_ref[...])
        scores(0, 0)
        accumulate(i, diag_slot)
        lam = _lambda(lq1_ref, lk1_ref, lq2_ref, lk2_ref, lambda_init)
        o_ref[...] = _diff_finish(acc_ref[0:e, :], acc_ref[e:e + 1, :], lam, g_ref[...],
                                  lambda_init).astype(o_ref.dtype)

    @pl.when((pl.program_id(0) == 0) & (pl.program_id(1) == 0) & (i == 0))
    def _():
        bias_ref[...] = jnp.where(_visible(bias_ref.shape, 0, 0), 0.0, MASK_VALUE)

    @pl.when(i == 0)
    def _():
        ones = jnp.ones((ONES_ROWS, blk), BF16)
        for j in range(n_blocks):
            rows = slice(j * blk, (j + 1) * blk)
            kb_ref[j] = k_ref[rows, :].astype(BF16)
            vt_ref[j] = jnp.concatenate([v_ref[rows, :].T.astype(BF16), ones], axis=0)
        q2_ref[...] = _stack_components(q_ref[...])
        scores(0, diag_slot, diagonal=True)
        finish()

    unroll = ATTN_LOOP_UNROLL

    def trip(j, carry):
        for u in range(unroll):
            scores(unroll * j + u + 1, (u + 1) % 2)
            accumulate(unroll * j + u, u % 2)
        return carry

    lax.fori_loop(0, jnp.maximum(i - 1, 0) // unroll, trip, 0)

    for rest in range(1, unroll + 1):
        @pl.when((i >= 1) & ((i - 1) % unroll == rest - 1))
        def _(rest=rest):
            first = i - rest
            for u in range(rest):
                if u + 1 < rest:
                    scores(first + u + 1, (u + 1) % 2)
                else:
                    scores(i, diag_slot, diagonal=True)
                accumulate(first + u, u % 2)
            finish()


def _attn_prompt(q, k, v, lams, gain_col, n_batch, seq_len, n_heads, lambda_init):
    t, width = q.shape
    e = width // n_heads
    blk = ATTN_BLOCK
    assert seq_len % blk == 0 and blk % CHUNK == 0 and e == V7X_LANES
    nq = seq_len // blk
    lam_spec = pl.BlockSpec(lams[0].shape, lambda b, h, i: (0, 0))
    n_blocks = seq_len // blk
    scratch = [((n_blocks, blk, e), BF16), ((n_blocks, e + ONES_ROWS, blk), BF16), ((2 * blk, e), BF16),
               ((3, blk, 2 * blk), F32), ((3, 1, 2 * blk), F32), ((blk, 2 * blk), F32), ((1, 2 * blk), F32),
               ((e + ONES_ROWS, 2 * blk), F32)]
    vmem = _vmem_limit(
        pipelined=[((blk, e), BF16)] * 3 + [((seq_len, e), F32)] * 2,
        resident=scratch + [((blk, 2 * blk), F32)] * 6)
    return pl.pallas_call(
        functools.partial(_attn_prompt_kernel, lambda_init=lambda_init),
        name="attn_prompt",
        grid=(n_batch, n_heads, nq),
        in_specs=[
            pl.BlockSpec((blk, e), lambda b, h, i: (b * nq + i, h)),
            pl.BlockSpec((blk, e), lambda b, h, i: (b * nq + jnp.minimum(i + 1, nq - 1), h)),
            pl.BlockSpec((seq_len, e), lambda b, h, i: (b, h)),
            pl.BlockSpec((seq_len, e), lambda b, h, i: (b, h)),
            lam_spec, lam_spec, lam_spec, lam_spec,
            pl.BlockSpec(gain_col.shape, lambda b, h, i: (0, 0)),
        ],
        out_specs=pl.BlockSpec((blk, e), lambda b, h, i: (b * nq + i, h)),
        out_shape=jax.ShapeDtypeStruct((t, width), BF16),
        scratch_shapes=[pltpu.VMEM(s, d) for s, d in scratch],
        compiler_params=pltpu.CompilerParams(
            dimension_semantics=("arbitrary", "arbitrary", "arbitrary"), vmem_limit_bytes=vmem),
    )(q, q, k, v, *lams, gain_col)


def _attn_sample_kernel(q_ref, kn_ref, vn_ref, kc_ref, vc_ref, lq1_ref, lk1_ref, lq2_ref, lk2_ref,
                        g_ref, o_ref, *, n_heads, past_len, lambda_init):
    rows = q_ref.shape[0]
    e = q_ref.shape[1] // n_heads
    lam = _lambda(lq1_ref, lk1_ref, lq2_ref, lk2_ref, lambda_init)
    gain_col = g_ref[...]
    visible = _visible((rows, 2 * rows), past_len, past_len)
    for hd in range(n_heads):
        cols = slice(hd * e, (hd + 1) * e)
        kc = kc_ref[pl.ds(hd, past_len, stride=n_heads), :].astype(BF16)
        vc = vc_ref[pl.ds(hd, past_len, stride=n_heads), :].astype(BF16)
        q2 = _stack_components(q_ref[:, cols])
        s_c = _scores(kc, q2)
        s_n = jnp.where(visible, _scores(kn_ref[:, cols].astype(BF16), q2), MASK_VALUE)
        m = jnp.maximum(jnp.max(s_c, axis=0, keepdims=True), jnp.max(s_n, axis=0, keepdims=True))
        p_c = jnp.exp(s_c - m)
        p_n = jnp.exp(s_n - m)
        l = jnp.sum(p_c, axis=0, keepdims=True) + jnp.sum(p_n, axis=0, keepdims=True)
        acc = (_weighted_values(vc, p_c.astype(BF16))
               + _weighted_values(vn_ref[:, cols].astype(BF16), p_n.astype(BF16)))
        o_ref[:, cols] = _diff_finish(acc, l, lam, gain_col, lambda_init).astype(o_ref.dtype)


def _attn_sample(q, k_new, v_new, k_cache, v_cache, lams, gain_col, n_batch, seq_len, n_heads,
                 lambda_init):
    t, width = q.shape
    e = width // n_heads
    past_len = k_cache.shape[0] // (n_batch * n_heads)
    assert seq_len % 8 == 0 and past_len % 8 == 0 and k_cache.shape[1] == e
    tok = lambda b: (b, 0)
    lam_spec = pl.BlockSpec(lams[0].shape, lambda b: (0, 0))
    vmem = _vmem_limit(
        pipelined=[((past_len * n_heads, e), F32)] * 2 + [((seq_len, width), F32)] * 4,
        resident=[((past_len, 2 * seq_len), F32)] * 4 + [((past_len, e), F32)] * 4)
    return pl.pallas_call(
        functools.partial(_attn_sample_kernel, n_heads=n_heads, past_len=past_len,
                          lambda_init=lambda_init),
        name="attn_sample",
        grid=(n_batch,),
        in_specs=[
            pl.BlockSpec((seq_len, width), tok),
            pl.BlockSpec((seq_len, width), tok),
            pl.BlockSpec((seq_len, width), tok),
            pl.BlockSpec((past_len * n_heads, e), tok),
            pl.BlockSpec((past_len * n_heads, e), tok),
            lam_spec, lam_spec, lam_spec, lam_spec,
            pl.BlockSpec(gain_col.shape, lambda b: (0, 0)),
        ],
        out_specs=pl.BlockSpec((seq_len, width), tok),
        out_shape=jax.ShapeDtypeStruct((t, width), BF16),
        compiler_params=pltpu.CompilerParams(
            dimension_semantics=("parallel",), vmem_limit_bytes=vmem),
    )(q, k_new, v_new, k_cache, v_cache, *lams, gain_col)


def _rope_tables(pos, head_dim, rows):
    half = head_dim // 2
    inv = ROPE_THETA ** (-jnp.arange(0, head_dim, 2, dtype=F32) / head_dim)
    ang = pos.astype(F32)[:, None] * inv[None, :]
    cos, sin = jnp.cos(ang), jnp.sin(ang)
    zero = jnp.zeros_like(sin)
    reps = V7X_LANES // head_dim
    cos_t = jnp.tile(jnp.concatenate([cos, cos], axis=1), (rows // pos.shape[0], reps))
    sin_lo = jnp.tile(jnp.concatenate([-sin, zero], axis=1), (rows // pos.shape[0], reps))
    sin_hi = jnp.tile(jnp.concatenate([zero, sin], axis=1), (rows // pos.shape[0], reps))
    return cos_t, sin_lo, sin_hi


def kernel(x_prompt, x_sample, cache_k, cache_v, state_conv, ffn1_norm, ffn1_w_gate, ffn1_w_up,
           ffn1_w_down, mix_norm, w_in, lambda_q1, lambda_k1, lambda_q2, lambda_k2, subln_gain,
           conv_w, w_out, ffn2_norm, ffn2_w_gate, ffn2_w_up, ffn2_w_down, final_norm):
    n_bp, seq_p, d = x_prompt.shape
    n_bs, seq_s, _ = x_sample.shape
    depth, _, past_len, n_heads, key_dim = cache_k.shape
    value_dim = cache_v.shape[-1]
    head_dim = key_dim // 2
    conv_ch = conv_w.shape[-1]
    attn_width = n_heads * key_dim
    assert key_dim == V7X_LANES and value_dim == V7X_LANES and attn_width == conv_ch
    q_scale = head_dim ** -0.5

    tabs_p = _rope_tables(jnp.arange(seq_p), head_dim, seq_p)
    tabs_s = _rope_tables(past_len + jnp.arange(seq_s), head_dim, max(seq_s, INPROJ_TOKEN_TILE))

    xp = x_prompt.reshape(n_bp * seq_p, d)
    xs = x_sample.reshape(n_bs * seq_s, d)
    row = lambda a: a.reshape(1, -1)
    outs = {name: [] for name in ("kp", "vp", "cp", "ks", "vs", "cs")}
    for l in range(depth):
        lambda_init = 0.8 - 0.6 * math.exp(-0.3 * l)
        last = l == depth - 1
        w1g, w1u, w1d = (_to_bf16(w[l]) for w in (ffn1_w_gate, ffn1_w_up, ffn1_w_down))
        w2g, w2u, w2d = (_to_bf16(w[l]) for w in (ffn2_w_gate, ffn2_w_up, ffn2_w_down))
        w_in_l, w_out_l = _to_bf16(w_in[l]), _to_bf16(w_out[l])
        lams = tuple(row(a[l]) for a in (lambda_q1, lambda_k1, lambda_q2, lambda_k2))
        gain = subln_gain[l].reshape(-1, 1)
        fin = row(final_norm)
        zeros_state = jnp.zeros((n_bp, conv_w.shape[1] - 1, conv_ch), F32)

        def layer(x, tabs, state, seq_len, attend):
            x = _ffn(x, row(ffn1_norm[l]), w1g, w1u, w1d, fin, False)
            q, k, v, conv_out, new_state = _inproj(
                x, row(mix_norm[l]), w_in_l, tabs, conv_w[l], state, seq_len, n_heads, q_scale)
            attn_out = attend(q, k, v)
            x = _outproj(x, attn_out, conv_out, w_out_l)
            x = _ffn(x, row(ffn2_norm[l]), w2g, w2u, w2d, fin, last)
            return x, k, v, new_state

        xp, kp, vp, cp = layer(
            xp, tabs_p, zeros_state, seq_p,
            lambda q, k, v: _attn_prompt(q, k, v, lams, gain, n_bp, seq_p, n_heads, lambda_init))
        kc = cache_k[l].reshape(n_bs * past_len * n_heads, key_dim)
        vc = cache_v[l].reshape(n_bs * past_len * n_heads, value_dim)
        xs, ks, vs, cs = layer(
            xs, tabs_s, state_conv[l], seq_s,
            lambda q, k, v: _attn_sample(q, k, v, kc, vc, lams, gain, n_bs, seq_s, n_heads,
                                         lambda_init))
        outs["kp"].append(kp.reshape(n_bp, seq_p, n_heads, key_dim))
        outs["vp"].append(vp.reshape(n_bp, seq_p, n_heads, value_dim))
        outs["cp"].append(cp)
        outs["ks"].append(ks.reshape(n_bs, seq_s, n_heads, key_dim))
        outs["vs"].append(vs.reshape(n_bs, seq_s, n_heads, value_dim))
        outs["cs"].append(cs)
    return (xp.reshape(x_prompt.shape), xs.reshape(x_sample.shape),
            jnp.stack(outs["kp"]), jnp.stack(outs["vp"]), jnp.stack(outs["cp"]),
            jnp.stack(outs["ks"]), jnp.stack(outs["vs"]), jnp.stack(outs["cs"]))
```

```python
import functools
import math

import jax
import jax.numpy as jnp
from jax import lax
from jax.experimental import pallas as pl
from jax.experimental.pallas import tpu as pltpu

F32 = jnp.float32
BF16 = jnp.bfloat16

CHUNK = 64
ROPE_THETA = 10000.0
EPS = 1e-6
MASK_VALUE = -1e30

V7X_LANES = 128
V7X_VMEM_BYTES = 64 * 1024 * 1024
VMEM_BUDGET_BYTES = V7X_VMEM_BYTES * 7 // 8

CAST_BLOCK_BYTES = 6 * 1024 * 1024
FFN_TOKEN_TILE = 512
FFN_COL_TILE = 512
INPROJ_TOKEN_TILE = 256
OUTPROJ_TOKEN_TILE = 512
ATTN_BLOCK = 512
ATTN_LOOP_UNROLL = 4
ONES_ROWS = 16
assert CHUNK & (CHUNK - 1) == 0 and ATTN_LOOP_UNROLL % 2 == 0


def _nbytes(shape, dtype):
    return math.prod(shape) * jnp.dtype(dtype).itemsize


def _vmem_limit(pipelined, resident):
    need = 2 * sum(_nbytes(s, d) for s, d in pipelined) + sum(_nbytes(s, d) for s, d in resident)
    return int(min(VMEM_BUDGET_BYTES, max(need, 16 * 1024 * 1024)))


def _rms(x, gain):
    return (x * lax.rsqrt(jnp.mean(x * x, axis=-1, keepdims=True) + EPS)) * gain


def _cast_kernel(x_ref, o_ref):
    o_ref[...] = x_ref[...].astype(o_ref.dtype)


def _to_bf16(w):
    r, c = w.shape
    slab = r
    while slab % 32 == 0 and _nbytes((slab, c), w.dtype) > CAST_BLOCK_BYTES:
        slab //= 2
    return pl.pallas_call(
        _cast_kernel,
        name="cast",
        grid=(r // slab,),
        in_specs=[pl.BlockSpec((slab, c), lambda i: (i, 0))],
        out_specs=pl.BlockSpec((slab, c), lambda i: (i, 0)),
        out_shape=jax.ShapeDtypeStruct(w.shape, BF16),
        compiler_params=pltpu.CompilerParams(
            dimension_semantics=("parallel",), vmem_limit_bytes=VMEM_BUDGET_BYTES),
    )(w)


def _ffn_kernel(x_ref, nw_ref, wg0_ref, wg1_ref, wu0_ref, wu1_ref, wd0_ref, wd1_ref, fn_ref,
                o_ref, h_ref, *, n_chunks, apply_final_norm):
    f = pl.program_id(1)

    @pl.when(f == 0)
    def _():
        h_ref[...] = _rms(x_ref[...], nw_ref[...]).astype(BF16)
        o_ref[...] = jnp.zeros_like(o_ref)

    def chunk(wg_ref, wu_ref, wd_ref):
        h = h_ref[...]
        gate = jnp.dot(h, wg_ref[...], preferred_element_type=F32)
        up = jnp.dot(h, wu_ref[...], preferred_element_type=F32)
        act = ((gate * jax.nn.sigmoid(gate)) * up).astype(BF16)
        return jnp.dot(act, wd_ref[...], preferred_element_type=F32)

    def pair():
        o_ref[...] += chunk(wg0_ref, wu0_ref, wd0_ref) + chunk(wg1_ref, wu1_ref, wd1_ref)

    if n_chunks % 2:
        @pl.when(f == 0)
        def _():
            o_ref[...] += chunk(wg0_ref, wu0_ref, wd0_ref)

        pl.when(f > 0)(pair)
    else:
        pair()

    @pl.when(f == pl.num_programs(1) - 1)
    def _():
        y = x_ref[...] + 0.5 * o_ref[...]
        if apply_final_norm:
            y = _rms(y, fn_ref[...])
        o_ref[...] = y


def _ffn(x, norm_w, w_gate, w_up, w_down, final_w, apply_final_norm):
    t, d = x.shape
    ffn = w_gate.shape[1]
    tm, tf = min(FFN_TOKEN_TILE, t), FFN_COL_TILE
    if t == 2 * FFN_TOKEN_TILE:
        tm, tf = t, FFN_COL_TILE // 2
    assert t % tm == 0 and ffn % tf == 0
    n_chunks = ffn // tf
    assert n_chunks >= 2
    if n_chunks % 2:
        first = lambda f: jnp.maximum(2 * f - 1, 0)
        second = lambda f: jnp.maximum(2 * f, 2)
    else:
        first = lambda f: 2 * f
        second = lambda f: 2 * f + 1
    tok = lambda i, f: (i, 0)
    const = lambda i, f: (0, 0)
    col0 = pl.BlockSpec((d, tf), lambda i, f: (0, first(f)))
    col1 = pl.BlockSpec((d, tf), lambda i, f: (0, second(f)))
    row0 = pl.BlockSpec((tf, d), lambda i, f: (first(f), 0))
    row1 = pl.BlockSpec((tf, d), lambda i, f: (second(f), 0))
    vmem = _vmem_limit(
        pipelined=[((tm, d), F32), ((tm, d), F32)] + [((d, tf), BF16)] * 6,
        resident=[((tm, d), BF16), ((tm, d), F32)] + [((tm, tf), F32)] * 6)
    return pl.pallas_call(
        functools.partial(_ffn_kernel, n_chunks=n_chunks, apply_final_norm=apply_final_norm),
        name="ffn",
        grid=(t // tm, pl.cdiv(n_chunks, 2)),
        in_specs=[pl.BlockSpec((tm, d), tok), pl.BlockSpec((1, d), const),
                  col0, col1, col0, col1, row0, row1, pl.BlockSpec((1, d), const)],
        out_specs=pl.BlockSpec((tm, d), tok),
        out_shape=jax.ShapeDtypeStruct((t, d), F32),
        scratch_shapes=[pltpu.VMEM((tm, d), BF16)],
        compiler_params=pltpu.CompilerParams(
            dimension_semantics=("parallel", "arbitrary"), vmem_limit_bytes=vmem),
    )(x, norm_w, w_gate, w_gate, w_up, w_up, w_down, w_down, final_w)


def _outproj_kernel(x_ref, a_ref, c_ref, w_ref, o_ref):
    mixed = jnp.concatenate([a_ref[...], c_ref[...]], axis=1)
    o_ref[...] = x_ref[...] + jnp.dot(mixed, w_ref[...], preferred_element_type=F32)


def _outproj(x, attn_out, conv_out, w_out):
    t, d = x.shape
    wa, wc = attn_out.shape[1], conv_out.shape[1]
    tm = min(OUTPROJ_TOKEN_TILE, t)
    assert t % tm == 0 and wa + wc == w_out.shape[0]
    tok = lambda i: (i, 0)
    vmem = _vmem_limit(
        pipelined=[((tm, d), F32), ((tm, wa), BF16), ((tm, wc), BF16), (w_out.shape, BF16),
                   ((tm, d), F32)],
        resident=[((tm, wa + wc), BF16), ((tm, d), F32)])
    return pl.pallas_call(
        _outproj_kernel,
        name="outproj",
        grid=(t // tm,),
        in_specs=[
            pl.BlockSpec((tm, d), tok),
            pl.BlockSpec((tm, wa), tok),
            pl.BlockSpec((tm, wc), tok),
            pl.BlockSpec(w_out.shape, lambda i: (0, 0)),
        ],
        out_specs=pl.BlockSpec((tm, d), tok),
        out_shape=jax.ShapeDtypeStruct((t, d), F32),
        compiler_params=pltpu.CompilerParams(
            dimension_semantics=("parallel",), vmem_limit_bytes=vmem),
    )(x, attn_out, conv_out, w_out)


def _inproj_kernel(x_ref, nw_ref, w_ref, cos_ref, sin_lo_ref, sin_hi_ref, cw_ref, st_ref,
                   q_ref, k_ref, v_ref, co_ref, sto_ref, carry_ref,
                   *, width, n_heads, n_seq, tiles_per_seq, q_scale):
    i = pl.program_id(0)
    tm = x_ref.shape[0]
    seq_rows = tm // n_seq
    if tiles_per_seq > 1:
        @pl.when(i % tiles_per_seq == 0)
        def _():
            carry_ref[0:2, :] = st_ref[0]

    h = _rms(x_ref[...], nw_ref[...]).astype(BF16)

    def proj(col):
        return jnp.dot(h, w_ref[:, col * width:(col + 1) * width], preferred_element_type=F32)

    def rope_store(x, out_ref, scale):
        cos, sin_lo, sin_hi = cos_ref[...], sin_lo_ref[...], sin_hi_ref[...]
        for hd in range(n_heads):
            xh = x[:, hd * V7X_LANES:(hd + 1) * V7X_LANES]
            r = (xh * cos + pltpu.roll(xh, V7X_LANES - 32, axis=1) * sin_lo
                 + pltpu.roll(xh, 32, axis=1) * sin_hi)
            if scale is not None:
                r = r * scale
            out_ref[:, hd * V7X_LANES:(hd + 1) * V7X_LANES] = r.astype(out_ref.dtype)

    rope_store(proj(0), q_ref, q_scale)
    rope_store(proj(1), k_ref, None)
    v_ref[...] = proj(2)

    cb = proj(3)
    u = proj(4) * proj(5)
    for s in range(n_seq):
        rows = slice(s * seq_rows, (s + 1) * seq_rows)
        us = u[rows]
        if tiles_per_seq > 1:
            hist0, hist1 = carry_ref[0:1, :], carry_ref[1:2, :]
        else:
            hist0, hist1 = st_ref[s, 0:1, :], st_ref[s, 1:2, :]
        row = lax.broadcasted_iota(jnp.int32, us.shape, 0)
        back1 = pltpu.roll(us, 1, axis=0)
        back2 = pltpu.roll(us, 2, axis=0)
        u1 = jnp.where(row == 0, hist1, back1)
        u2 = jnp.where(row == 0, hist0, jnp.where(row == 1, hist1, back2))
        conv = cw_ref[0:1, :] * u2 + cw_ref[1:2, :] * u1 + cw_ref[2:3, :] * us
        co_ref[rows, :] = (cb[rows] * conv).astype(co_ref.dtype)
        last = back2[0:2, :]
        sto_ref[s] = last
        if tiles_per_seq > 1:
            carry_ref[0:2, :] = last


def _inproj(x, norm_w, w_in, rope_tabs, conv_w, conv_state, seq_len, n_heads, q_scale):
    t, d = x.shape
    width = w_in.shape[1] // 6
    n_batch = conv_state.shape[0]
    hist = conv_state.shape[1]
    assert hist == 2 and seq_len >= hist and t == n_batch * seq_len
    tm = INPROJ_TOKEN_TILE
    if seq_len >= tm:
        assert seq_len % tm == 0
        n_seq, tiles_per_seq = 1, seq_len // tm
    else:
        assert tm % seq_len == 0 and seq_len % 8 == 0
        n_seq, tiles_per_seq = tm // seq_len, 1
    cos, sin_lo, sin_hi = rope_tabs
    pos_tiles = cos.shape[0] // tm
    tab_spec = pl.BlockSpec((tm, V7X_LANES), lambda i: (i % pos_tiles, 0))
    tok = lambda i: (i, 0)
    const = lambda i: (0, 0)
    st_spec = pl.BlockSpec((n_seq, hist, width), lambda i: (i // tiles_per_seq, 0, 0))
    vmem = _vmem_limit(
        pipelined=[((tm, d), F32), ((tm, width), BF16), ((tm, width), F32), ((tm, width), F32),
                   ((tm, width), BF16), ((tm, V7X_LANES), F32), ((tm, V7X_LANES), F32),
                   ((tm, V7X_LANES), F32)],
        resident=[(w_in.shape, BF16), ((tm, d), BF16)] + [((tm, width), F32)] * 8)
    return pl.pallas_call(
        functools.partial(_inproj_kernel, width=width, n_heads=n_heads, n_seq=n_seq,
                          tiles_per_seq=tiles_per_seq, q_scale=q_scale),
        name="inproj",
        grid=(t // tm,),
        in_specs=[
            pl.BlockSpec((tm, d), tok),
            pl.BlockSpec((1, d), const),
            pl.BlockSpec(w_in.shape, const, pipeline_mode=pl.Buffered(1)),
            tab_spec, tab_spec, tab_spec,
            pl.BlockSpec(conv_w.shape, const),
            st_spec,
        ],
        out_specs=[
            pl.BlockSpec((tm, width), tok),
            pl.BlockSpec((tm, width), tok),
            pl.BlockSpec((tm, width), tok),
            pl.BlockSpec((tm, width), tok),
            st_spec,
        ],
        out_shape=[
            jax.ShapeDtypeStruct((t, width), BF16),
            jax.ShapeDtypeStruct((t, width), F32),
            jax.ShapeDtypeStruct((t, width), F32),
            jax.ShapeDtypeStruct((t, width), BF16),
            jax.ShapeDtypeStruct(conv_state.shape, F32),
        ],
        scratch_shapes=[pltpu.VMEM((8, width), F32)],
        compiler_params=pltpu.CompilerParams(
            dimension_semantics=("arbitrary",), vmem_limit_bytes=vmem),
    )(x, norm_w, w_in, cos, sin_lo, sin_hi, conv_w, conv_state)


def _stack_components(q):
    lane = lax.broadcasted_iota(jnp.int32, q.shape, 1)
    half = q.shape[1] // 2
    zero = jnp.zeros_like(q)
    return jnp.concatenate([jnp.where(lane < half, q, zero), jnp.where(lane >= half, q, zero)], axis=0)


def _visible(shape, q_start, k_start):
    assert CHUNK & (CHUNK - 1) == 0
    rows = shape[1] // 2
    c = lax.broadcasted_iota(jnp.int32, shape, 1)
    q_pos = q_start + jnp.where(c >= rows, c - rows, c)
    k_pos = k_start + lax.broadcasted_iota(jnp.int32, shape, 0)
    return k_pos < (q_pos | (CHUNK - 1)) + 1


def _scores(k, q2):
    return lax.dot_general(k, q2, (((1,), (1,)), ((), ())), preferred_element_type=F32)


def _weighted_values(v, p):
    return lax.dot_general(v, p, (((0,), (0,)), ((), ())), preferred_element_type=F32)


def _lambda(lq1_ref, lk1_ref, lq2_ref, lk2_ref, lambda_init):
    s1 = jnp.sum(lq1_ref[...] * lk1_ref[...], axis=1, keepdims=True)
    s2 = jnp.sum(lq2_ref[...] * lk2_ref[...], axis=1, keepdims=True)
    return jnp.exp(s1) - jnp.exp(s2) + lambda_init


def _diff_finish(acc, l, lam, gain_col, lambda_init):
    rows = acc.shape[1] // 2
    o = acc[:, :rows] / l[:, :rows] - lam * (acc[:, rows:] / l[:, rows:])
    y = (o * lax.rsqrt(jnp.mean(o * o, axis=0, keepdims=True) + EPS)) * gain_col
    return (y * (1.0 - lambda_init)).T


def _attn_prompt_kernel(q_ref, qn_ref, k_ref, v_ref, lq1_ref, lk1_ref, lq2_ref, lk2_ref, g_ref, o_ref,
                        kb_ref, vt_ref, q2_ref, s_ref, cmax_ref, bias_ref, m_ref, acc_ref, *, lambda_init):
    i = pl.program_id(2)
    blk = q_ref.shape[0]
    e = v_ref.shape[1]
    n_blocks = kb_ref.shape[0]
    diag_slot = 2

    def scores(kb, slot, diagonal=False):
        s = _scores(kb_ref[kb], q2_ref[...])
        if diagonal:
            s = s + bias_ref[...]
        s_ref[slot] = s
        cmax_ref[slot] = jnp.max(s, axis=0, keepdims=True)

    def accumulate(kb, slot):
        m_prev = m_ref[...]
        m_new = jnp.maximum(m_prev, cmax_ref[slot])
        alpha = jnp.exp(m_prev - m_new)
        p = jnp.exp(s_ref[slot] - m_new).astype(BF16)
        acc_ref[...] = alpha * acc_ref[...] + jnp.dot(vt_ref[kb], p, preferred_element_type=F32)
        m_ref[...] = m_new

    m_ref[...] = jnp.full_like(m_ref, MASK_VALUE)
    acc_ref[...] = jnp.zeros_like(acc_ref)

    def finish():
        q2_ref[...] = _stack_components(qn_ref[...])
        scores(0, 0)
        accumulate(i, diag_slot)
        lam = _lambda(lq1_ref, lk1_ref, lq2_ref, lk2_ref, lambda_init)
        o_ref[...] = _diff_finish(acc_ref[0:e, :], acc_ref[e:e + 1, :], lam, g_ref[...],
                                  lambda_init).astype(o_ref.dtype)

    @pl.when((pl.program_id(0) == 0) & (pl.program_id(1) == 0) & (i == 0))
    def _():
        bias_ref[...] = jnp.where(_visible(bias_ref.shape, 0, 0), 0.0, MASK_VALUE)

    @pl.when(i == 0)
    def _():
        ones = jnp.ones((ONES_ROWS, blk), BF16)
        for j in range(n_blocks):
            rows = slice(j * blk, (j + 1) * blk)
            kb_ref[j] = k_ref[rows, :].astype(BF16)
            vt_ref[j] = jnp.concatenate([v_ref[rows, :].T.astype(BF16), ones], axis=0)
        q2_ref[...] = _stack_components(q_ref[...])
        scores(0, diag_slot, diagonal=True)
        finish()

    unroll = ATTN_LOOP_UNROLL

    def trip(j, carry):
        for u in range(unroll):
            scores(unroll * j + u + 1, (u + 1) % 2)
            accumulate(unroll * j + u, u % 2)
        return carry

    lax.fori_loop(0, jnp.maximum(i - 1, 0) // unroll, trip, 0)

    for rest in range(1, unroll + 1):
        @pl.when((i >= 1) & ((i - 1) % unroll == rest - 1))
        def _(rest=rest):
            first = i - rest
            for u in range(rest):
                if u + 1 < rest:
                    scores(first + u + 1, (u + 1) % 2)
                else:
                    scores(i, diag_slot, diagonal=True)
                accumulate(first + u, u % 2)
            finish()


def _attn_prompt(q, k, v, lams, gain_col, n_batch, seq_len, n_heads, lambda_init):
    t, width = q.shape
    e = width // n_heads
    blk = ATTN_BLOCK
    assert seq_len % blk == 0 and blk % CHUNK == 0 and e == V7X_LANES
    nq = seq_len // blk
    lam_spec = pl.BlockSpec(lams[0].shape, lambda b, h, i: (0, 0))
    n_blocks = seq_len // blk
    scratch = [((n_blocks, blk, e), BF16), ((n_blocks, e + ONES_ROWS, blk), BF16), ((2 * blk, e), BF16),
               ((3, blk, 2 * blk), F32), ((3, 1, 2 * blk), F32), ((blk, 2 * blk), F32), ((1, 2 * blk), F32),
               ((e + ONES_ROWS, 2 * blk), F32)]
    vmem = _vmem_limit(
        pipelined=[((blk, e), BF16)] * 3 + [((seq_len, e), F32)] * 2,
        resident=scratch + [((blk, 2 * blk), F32)] * 6)
    return pl.pallas_call(
        functools.partial(_attn_prompt_kernel, lambda_init=lambda_init),
        name="attn_prompt",
        grid=(n_batch, n_heads, nq),
        in_specs=[
            pl.BlockSpec((blk, e), lambda b, h, i: (b * nq + i, h)),
            pl.BlockSpec((blk, e), lambda b, h, i: (b * nq + jnp.minimum(i + 1, nq - 1), h)),
            pl.BlockSpec((seq_len, e), lambda b, h, i: (b, h)),
            pl.BlockSpec((seq_len, e), lambda b, h, i: (b, h)),
            lam_spec, lam_spec, lam_spec, lam_spec,
            pl.BlockSpec(gain_col.shape, lambda b, h, i: (0, 0)),
        ],
        out_specs=pl.BlockSpec((blk, e), lambda b, h, i: (b * nq + i, h)),
        out_shape=jax.ShapeDtypeStruct((t, width), BF16),
        scratch_shapes=[pltpu.VMEM(s, d) for s, d in scratch],
        compiler_params=pltpu.CompilerParams(
            dimension_semantics=("arbitrary", "arbitrary", "arbitrary"), vmem_limit_bytes=vmem),
    )(q, q, k, v, *lams, gain_col)


def _attn_sample_kernel(q_ref, kn_ref, vn_ref, kc_ref, vc_ref, lq1_ref, lk1_ref, lq2_ref, lk2_ref,
                        g_ref, o_ref, *, n_heads, past_len, lambda_init):
    rows = q_ref.shape[0]
    e = q_ref.shape[1] // n_heads
    lam = _lambda(lq1_ref, lk1_ref, lq2_ref, lk2_ref, lambda_init)
    gain_col = g_ref[...]
    visible = _visible((rows, 2 * rows), past_len, past_len)
    for hd in range(n_heads):
        cols = slice(hd * e, (hd + 1) * e)
        kc = kc_ref[pl.ds(hd, past_len, stride=n_heads), :].astype(BF16)
        vc = vc_ref[pl.ds(hd, past_len, stride=n_heads), :].astype(BF16)
        q2 = _stack_components(q_ref[:, cols])
        s_c = _scores(kc, q2)
        s_n = jnp.where(visible, _scores(kn_ref[:, cols].astype(BF16), q2), MASK_VALUE)
        m = jnp.maximum(jnp.max(s_c, axis=0, keepdims=True), jnp.max(s_n, axis=0, keepdims=True))
        p_c = jnp.exp(s_c - m)
        p_n = jnp.exp(s_n - m)
        l = jnp.sum(p_c, axis=0, keepdims=True) + jnp.sum(p_n, axis=0, keepdims=True)
        acc = (_weighted_values(vc, p_c.astype(BF16))
               + _weighted_values(vn_ref[:, cols].astype(BF16), p_n.astype(BF16)))
        o_ref[:, cols] = _diff_finish(acc, l, lam, gain_col, lambda_init).astype(o_ref.dtype)


def _attn_sample(q, k_new, v_new, k_cache, v_cache, lams, gain_col, n_batch, seq_len, n_heads,
                 lambda_init):
    t, width = q.shape
    e = width // n_heads
    past_len = k_cache.shape[0] // (n_batch * n_heads)
    assert seq_len % 8 == 0 and past_len % 8 == 0 and k_cache.shape[1] == e
    tok = lambda b: (b, 0)
    lam_spec = pl.BlockSpec(lams[0].shape, lambda b: (0, 0))
    vmem = _vmem_limit(
        pipelined=[((past_len * n_heads, e), F32)] * 2 + [((seq_len, width), F32)] * 4,
        resident=[((past_len, 2 * seq_len), F32)] * 4 + [((past_len, e), F32)] * 4)
    return pl.pallas_call(
        functools.partial(_attn_sample_kernel, n_heads=n_heads, past_len=past_len,
                          lambda_init=lambda_init),
        name="attn_sample",
        grid=(n_batch,),
        in_specs=[
            pl.BlockSpec((seq_len, width), tok),
            pl.BlockSpec((seq_len, width), tok),
            pl.BlockSpec((seq_len, width), tok),
            pl.BlockSpec((past_len * n_heads, e), tok),
            pl.BlockSpec((past_len * n_heads, e), tok),
            lam_spec, lam_spec, lam_spec, lam_spec,
            pl.BlockSpec(gain_col.shape, lambda b: (0, 0)),
        ],
        out_specs=pl.BlockSpec((seq_len, width), tok),
        out_shape=jax.ShapeDtypeStruct((t, width), BF16),
        compiler_params=pltpu.CompilerParams(
            dimension_semantics=("parallel",), vmem_limit_bytes=vmem),
    )(q, k_new, v_new, k_cache, v_cache, *lams, gain_col)


def _rope_tables(pos, head_dim, rows):
    half = head_dim // 2
    inv = ROPE_THETA ** (-jnp.arange(0, head_dim, 2, dtype=F32) / head_dim)
    ang = pos.astype(F32)[:, None] * inv[None, :]
    cos, sin = jnp.cos(ang), jnp.sin(ang)
    zero = jnp.zeros_like(sin)
    reps = V7X_LANES // head_dim
    cos_t = jnp.tile(jnp.concatenate([cos, cos], axis=1), (rows // pos.shape[0], reps))
    sin_lo = jnp.tile(jnp.concatenate([-sin, zero], axis=1), (rows // pos.shape[0], reps))
    sin_hi = jnp.tile(jnp.concatenate([zero, sin], axis=1), (rows // pos.shape[0], reps))
    return cos_t, sin_lo, sin_hi


def kernel(x_prompt, x_sample, cache_k, cache_v, state_conv, ffn1_norm, ffn1_w_gate, ffn1_w_up,
           ffn1_w_down, mix_norm, w_in, lambda_q1, lambda_k1, lambda_q2, lambda_k2, subln_gain,
           conv_w, w_out, ffn2_norm, ffn2_w_gate, ffn2_w_up, ffn2_w_down, final_norm):
    n_bp, seq_p, d = x_prompt.shape
    n_bs, seq_s, _ = x_sample.shape
    depth, _, past_len, n_heads, key_dim = cache_k.shape
    value_dim = cache_v.shape[-1]
    head_dim = key_dim // 2
    conv_ch = conv_w.shape[-1]
    attn_width = n_heads * key_dim
    assert key_dim == V7X_LANES and value_dim == V7X_LANES and attn_width == conv_ch
    q_scale = head_dim ** -0.5

    tabs_p = _rope_tables(jnp.arange(seq_p), head_dim, seq_p)
    tabs_s = _rope_tables(past_len + jnp.arange(seq_s), head_dim, max(seq_s, INPROJ_TOKEN_TILE))

    xp = x_prompt.reshape(n_bp * seq_p, d)
    xs = x_sample.reshape(n_bs * seq_s, d)
    row = lambda a: a.reshape(1, -1)
    outs = {name: [] for name in ("kp", "vp", "cp", "ks", "vs", "cs")}
    for l in range(depth):
        lambda_init = 0.8 - 0.6 * math.exp(-0.3 * l)
        last = l == depth - 1
        w1g, w1u, w1d = (_to_bf16(w[l]) for w in (ffn1_w_gate, ffn1_w_up, ffn1_w_down))
        w2g, w2u, w2d = (_to_bf16(w[l]) for w in (ffn2_w_gate, ffn2_w_up, ffn2_w_down))
        w_in_l, w_out_l = _to_bf16(w_in[l]), _to_bf16(w_out[l])
        lams = tuple(row(a[l]) for a in (lambda_q1, lambda_k1, lambda_q2, lambda_k2))
        gain = subln_gain[l].reshape(-1, 1)
        fin = row(final_norm)
        zeros_state = jnp.zeros((n_bp, conv_w.shape[1] - 1, conv_ch), F32)

        def layer(x, tabs, state, seq_len, attend):
            x = _ffn(x, row(ffn1_norm[l]), w1g, w1u, w1d, fin, False)
            q, k, v, conv_out, new_state = _inproj(
                x, row(mix_norm[l]), w_in_l, tabs, conv_w[l], state, seq_len, n_heads, q_scale)
            attn_out = attend(q, k, v)
            x = _outproj(x, attn_out, conv_out, w_out_l)
            x = _ffn(x, row(ffn2_norm[l]), w2g, w2u, w2d, fin, last)
            return x, k, v, new_state

        xp, kp, vp, cp = layer(
            xp, tabs_p, zeros_state, seq_p,
            lambda q, k, v: _attn_prompt(q, k, v, lams, gain, n_bp, seq_p, n_heads, lambda_init))
        kc = cache_k[l].reshape(n_bs * past_len * n_heads, key_dim)
        vc = cache_v[l].reshape(n_bs * past_len * n_heads, value_dim)
        xs, ks, vs, cs = layer(
            xs, tabs_s, state_conv[l], seq_s,
            lambda q, k, v: _attn_sample(q, k, v, kc, vc, lams, gain, n_bs, seq_s, n_heads,
                                         lambda_init))
        outs["kp"].append(kp.reshape(n_bp, seq_p, n_heads, key_dim))
        outs["vp"].append(vp.reshape(n_bp, seq_p, n_heads, value_dim))
        outs["cp"].append(cp)
        outs["ks"].append(ks.reshape(n_bs, seq_s, n_heads, key_dim))
        outs["vs"].append(vs.reshape(n_bs, seq_s, n_heads, value_dim))
        outs["cs"].append(cs)
    return (xp.reshape(x_prompt.shape), xs.reshape(x_sample.shape),
            jnp.stack(outs["kp"]), jnp.stack(outs["vp"]), jnp.stack(outs["cp"]),
            jnp.stack(outs["ks"]), jnp.stack(outs["vs"]), jnp.stack(outs["cs"]))
```

```python
import functools
import math

import jax
import jax.numpy as jnp
from jax import lax
from jax.experimental import pallas as pl
from jax.experimental.pallas import tpu as pltpu

F32 = jnp.float32
BF16 = jnp.bfloat16

CHUNK = 64
ROPE_THETA = 10000.0
EPS = 1e-6
MASK_VALUE = -1e30

V7X_LANES = 128
V7X_VMEM_BYTES = 64 * 1024 * 1024
VMEM_BUDGET_BYTES = V7X_VMEM_BYTES * 7 // 8

CAST_BLOCK_BYTES = 6 * 1024 * 1024
FFN_TOKEN_TILE = 512
FFN_COL_TILE = 512
INPROJ_TOKEN_TILE = 256
OUTPROJ_TOKEN_TILE = 512
ATTN_BLOCK = 512
ATTN_LOOP_UNROLL = 4
ONES_ROWS = 16
assert CHUNK & (CHUNK - 1) == 0


def _nbytes(shape, dtype):
    return math.prod(shape) * jnp.dtype(dtype).itemsize


def _vmem_limit(pipelined, resident):
    need = 2 * sum(_nbytes(s, d) for s, d in pipelined) + sum(_nbytes(s, d) for s, d in resident)
    return int(min(VMEM_BUDGET_BYTES, max(need, 16 * 1024 * 1024)))


def _rms(x, gain):
    return (x * lax.rsqrt(jnp.mean(x * x, axis=-1, keepdims=True) + EPS)) * gain


def _cast_kernel(x_ref, o_ref):
    o_ref[...] = x_ref[...].astype(o_ref.dtype)


def _to_bf16(w):
    r, c = w.shape
    slab = r
    while slab % 32 == 0 and _nbytes((slab, c), w.dtype) > CAST_BLOCK_BYTES:
        slab //= 2
    return pl.pallas_call(
        _cast_kernel,
        name="cast",
        grid=(r // slab,),
        in_specs=[pl.BlockSpec((slab, c), lambda i: (i, 0))],
        out_specs=pl.BlockSpec((slab, c), lambda i: (i, 0)),
        out_shape=jax.ShapeDtypeStruct(w.shape, BF16),
        compiler_params=pltpu.CompilerParams(
            dimension_semantics=("parallel",), vmem_limit_bytes=VMEM_BUDGET_BYTES),
    )(w)


def _ffn_kernel(x_ref, nw_ref, wg0_ref, wg1_ref, wu0_ref, wu1_ref, wd0_ref, wd1_ref, fn_ref,
                o_ref, h_ref, *, n_chunks, apply_final_norm):
    f = pl.program_id(1)

    @pl.when(f == 0)
    def _():
        h_ref[...] = _rms(x_ref[...], nw_ref[...]).astype(BF16)
        if n_chunks % 2 == 0:
            o_ref[...] = jnp.zeros_like(o_ref)

    def chunk(wg_ref, wu_ref, wd_ref):
        h = h_ref[...]
        gate = jnp.dot(h, wg_ref[...], preferred_element_type=F32)
        up = jnp.dot(h, wu_ref[...], preferred_element_type=F32)
        act = ((gate * jax.nn.sigmoid(gate)) * up).astype(BF16)
        return jnp.dot(act, wd_ref[...], preferred_element_type=F32)

    def pair():
        o_ref[...] += chunk(wg0_ref, wu0_ref, wd0_ref) + chunk(wg1_ref, wu1_ref, wd1_ref)

    if n_chunks % 2:
        @pl.when(f == 0)
        def _():
            o_ref[...] = chunk(wg0_ref, wu0_ref, wd0_ref)

        pl.when(f > 0)(pair)
    else:
        pair()

    @pl.when(f == pl.num_programs(1) - 1)
    def _():
        y = x_ref[...] + 0.5 * o_ref[...]
        if apply_final_norm:
            y = _rms(y, fn_ref[...])
        o_ref[...] = y


def _ffn(x, norm_w, w_gate, w_up, w_down, final_w, apply_final_norm):
    t, d = x.shape
    ffn = w_gate.shape[1]
    tm, tf = min(FFN_TOKEN_TILE, t), FFN_COL_TILE
    if t == 2 * FFN_TOKEN_TILE:
        tm, tf = t, FFN_COL_TILE // 2
    assert t % tm == 0 and ffn % tf == 0
    n_chunks = ffn // tf
    assert n_chunks >= 2
    if n_chunks % 2:
        first = lambda f: jnp.maximum(2 * f - 1, 0)
        second = lambda f: jnp.maximum(2 * f, 2)
    else:
        first = lambda f: 2 * f
        second = lambda f: 2 * f + 1
    tok = lambda i, f: (i, 0)
    const = lambda i, f: (0, 0)
    col0 = pl.BlockSpec((d, tf), lambda i, f: (0, first(f)))
    col1 = pl.BlockSpec((d, tf), lambda i, f: (0, second(f)))
    row0 = pl.BlockSpec((tf, d), lambda i, f: (first(f), 0))
    row1 = pl.BlockSpec((tf, d), lambda i, f: (second(f), 0))
    vmem = _vmem_limit(
        pipelined=[((tm, d), F32), ((tm, d), F32)] + [((d, tf), BF16)] * 6,
        resident=[((tm, d), BF16), ((tm, d), F32)] + [((tm, tf), F32)] * 6)
    return pl.pallas_call(
        functools.partial(_ffn_kernel, n_chunks=n_chunks, apply_final_norm=apply_final_norm),
        name="ffn",
        grid=(t // tm, pl.cdiv(n_chunks, 2)),
        in_specs=[pl.BlockSpec((tm, d), tok), pl.BlockSpec((1, d), const),
                  col0, col1, col0, col1, row0, row1, pl.BlockSpec((1, d), const)],
        out_specs=pl.BlockSpec((tm, d), tok),
        out_shape=jax.ShapeDtypeStruct((t, d), F32),
        scratch_shapes=[pltpu.VMEM((tm, d), BF16)],
        compiler_params=pltpu.CompilerParams(
            dimension_semantics=("parallel", "arbitrary"), vmem_limit_bytes=vmem),
    )(x, norm_w, w_gate, w_gate, w_up, w_up, w_down, w_down, final_w)


def _outproj_kernel(x_ref, a_ref, c_ref, w_ref, o_ref):
    mixed = jnp.concatenate([a_ref[...], c_ref[...]], axis=1)
    o_ref[...] = x_ref[...] + jnp.dot(mixed, w_ref[...], preferred_element_type=F32)


def _outproj(x, attn_out, conv_out, w_out):
    t, d = x.shape
    wa, wc = attn_out.shape[1], conv_out.shape[1]
    tm = min(OUTPROJ_TOKEN_TILE, t)
    assert t % tm == 0 and wa + wc == w_out.shape[0]
    tok = lambda i: (i, 0)
    vmem = _vmem_limit(
        pipelined=[((tm, d), F32), ((tm, wa), BF16), ((tm, wc), BF16), (w_out.shape, BF16),
                   ((tm, d), F32)],
        resident=[((tm, wa + wc), BF16), ((tm, d), F32)])
    return pl.pallas_call(
        _outproj_kernel,
        name="outproj",
        grid=(t // tm,),
        in_specs=[
            pl.BlockSpec((tm, d), tok),
            pl.BlockSpec((tm, wa), tok),
            pl.BlockSpec((tm, wc), tok),
            pl.BlockSpec(w_out.shape, lambda i: (0, 0)),
        ],
        out_specs=pl.BlockSpec((tm, d), tok),
        out_shape=jax.ShapeDtypeStruct((t, d), F32),
        compiler_params=pltpu.CompilerParams(
            dimension_semantics=("parallel",), vmem_limit_bytes=vmem),
    )(x, attn_out, conv_out, w_out)


def _inproj_kernel(x_ref, nw_ref, w_ref, cos_ref, sin_lo_ref, sin_hi_ref, cw_ref, st_ref,
                   q_ref, k_ref, v_ref, co_ref, sto_ref, carry_ref,
                   *, width, n_heads, n_seq, tiles_per_seq, q_scale):
    i = pl.program_id(0)
    tm = x_ref.shape[0]
    seq_rows = tm // n_seq
    h = _rms(x_ref[...], nw_ref[...]).astype(BF16)

    def proj(col):
        return jnp.dot(h, w_ref[:, col * width:(col + 1) * width], preferred_element_type=F32)

    def rope_store(x, out_ref, scale):
        cos, sin_lo, sin_hi = cos_ref[...], sin_lo_ref[...], sin_hi_ref[...]
        for hd in range(n_heads):
            xh = x[:, hd * V7X_LANES:(hd + 1) * V7X_LANES]
            r = (xh * cos + pltpu.roll(xh, V7X_LANES - 32, axis=1) * sin_lo
                 + pltpu.roll(xh, 32, axis=1) * sin_hi)
            if scale is not None:
                r = r * scale
            out_ref[:, hd * V7X_LANES:(hd + 1) * V7X_LANES] = r.astype(out_ref.dtype)

    rope_store(proj(0), q_ref, q_scale)
    rope_store(proj(1), k_ref, None)
    v_ref[...] = proj(2)

    cb = proj(3)
    u = proj(4) * proj(5)
    if tiles_per_seq > 1:
        @pl.when(i % tiles_per_seq == 0)
        def _():
            carry_ref[0:2, :] = st_ref[0]
    for s in range(n_seq):
        rows = slice(s * seq_rows, (s + 1) * seq_rows)
        us = u[rows]
        if tiles_per_seq > 1:
            hist0, hist1 = carry_ref[0:1, :], carry_ref[1:2, :]
        else:
            hist0, hist1 = st_ref[s, 0:1, :], st_ref[s, 1:2, :]
        row = lax.broadcasted_iota(jnp.int32, us.shape, 0)
        back1 = pltpu.roll(us, 1, axis=0)
        back2 = pltpu.roll(us, 2, axis=0)
        u1 = jnp.where(row == 0, hist1, back1)
        u2 = jnp.where(row == 0, hist0, jnp.where(row == 1, hist1, back2))
        conv = cw_ref[0:1, :] * u2 + cw_ref[1:2, :] * u1 + cw_ref[2:3, :] * us
        co_ref[rows, :] = (cb[rows] * conv).astype(co_ref.dtype)
        last = back2[0:2, :]
        sto_ref[s] = last
        if tiles_per_seq > 1:
            carry_ref[0:2, :] = last


def _inproj(x, norm_w, w_in, rope_tabs, conv_w, conv_state, seq_len, n_heads, q_scale):
    t, d = x.shape
    width = w_in.shape[1] // 6
    n_batch = conv_state.shape[0]
    hist = conv_state.shape[1]
    assert hist == 2 and seq_len >= hist and t == n_batch * seq_len
    tm = INPROJ_TOKEN_TILE
    if seq_len >= tm:
        assert seq_len % tm == 0
        n_seq, tiles_per_seq = 1, seq_len // tm
    else:
        assert tm % seq_len == 0 and seq_len % 8 == 0
        n_seq, tiles_per_seq = tm // seq_len, 1
    cos, sin_lo, sin_hi = rope_tabs
    pos_tiles = cos.shape[0] // tm
    tab_spec = pl.BlockSpec((tm, V7X_LANES), lambda i: (i % pos_tiles, 0))
    tok = lambda i: (i, 0)
    const = lambda i: (0, 0)
    st_spec = pl.BlockSpec((n_seq, hist, width), lambda i: (i // tiles_per_seq, 0, 0))
    vmem = _vmem_limit(
        pipelined=[((tm, d), F32), ((tm, width), BF16), ((tm, width), F32), ((tm, width), F32),
                   ((tm, width), BF16), ((tm, V7X_LANES), F32), ((tm, V7X_LANES), F32),
                   ((tm, V7X_LANES), F32)],
        resident=[(w_in.shape, BF16), ((tm, d), BF16)] + [((tm, width), F32)] * 8)
    return pl.pallas_call(
        functools.partial(_inproj_kernel, width=width, n_heads=n_heads, n_seq=n_seq,
                          tiles_per_seq=tiles_per_seq, q_scale=q_scale),
        name="inproj",
        grid=(t // tm,),
        in_specs=[
            pl.BlockSpec((tm, d), tok),
            pl.BlockSpec((1, d), const),
            pl.BlockSpec(w_in.shape, const, pipeline_mode=pl.Buffered(1)),
            tab_spec, tab_spec, tab_spec,
            pl.BlockSpec(conv_w.shape, const),
            st_spec,
        ],
        out_specs=[
            pl.BlockSpec((tm, width), tok),
            pl.BlockSpec((tm, width), tok),
            pl.BlockSpec((tm, width), tok),
            pl.BlockSpec((tm, width), tok),
            st_spec,
        ],
        out_shape=[
            jax.ShapeDtypeStruct((t, width), BF16),
            jax.ShapeDtypeStruct((t, width), F32),
            jax.ShapeDtypeStruct((t, width), F32),
            jax.ShapeDtypeStruct((t, width), BF16),
            jax.ShapeDtypeStruct(conv_state.shape, F32),
        ],
        scratch_shapes=[pltpu.VMEM((8, width), F32)],
        compiler_params=pltpu.CompilerParams(
            dimension_semantics=("arbitrary",), vmem_limit_bytes=vmem),
    )(x, norm_w, w_in, cos, sin_lo, sin_hi, conv_w, conv_state)


def _stack_components(q):
    lane = lax.broadcasted_iota(jnp.int32, q.shape, 1)
    half = q.shape[1] // 2
    zero = jnp.zeros_like(q)
    return jnp.concatenate([jnp.where(lane < half, q, zero), jnp.where(lane >= half, q, zero)], axis=0)


def _visible(shape, q_start, k_start):
    assert CHUNK & (CHUNK - 1) == 0
    rows = shape[1] // 2
    c = lax.broadcasted_iota(jnp.int32, shape, 1)
    q_pos = q_start + jnp.where(c >= rows, c - rows, c)
    k_pos = k_start + lax.broadcasted_iota(jnp.int32, shape, 0)
    return k_pos < (q_pos | (CHUNK - 1)) + 1


def _scores(k, q2):
    return lax.dot_general(k, q2, (((1,), (1,)), ((), ())), preferred_element_type=F32)


def _weighted_values(v, p):
    return lax.dot_general(v, p, (((0,), (0,)), ((), ())), preferred_element_type=F32)


def _lambda(lq1_ref, lk1_ref, lq2_ref, lk2_ref, lambda_init):
    s1 = jnp.sum(lq1_ref[...] * lk1_ref[...], axis=1, keepdims=True)
    s2 = jnp.sum(lq2_ref[...] * lk2_ref[...], axis=1, keepdims=True)
    return jnp.exp(s1) - jnp.exp(s2) + lambda_init


def _diff_finish(acc, l, lam, gain_col, lambda_init):
    rows = acc.shape[1] // 2
    o = acc[:, :rows] / l[:, :rows] - lam * (acc[:, rows:] / l[:, rows:])
    y = (o * lax.rsqrt(jnp.mean(o * o, axis=0, keepdims=True) + EPS)) * gain_col
    return (y * (1.0 - lambda_init)).T


def _attn_prompt_kernel(q_ref, qn_ref, k_ref, v_ref, lq1_ref, lk1_ref, lq2_ref, lk2_ref, g_ref, o_ref,
                        kb_ref, vt_ref, q2_ref, s_ref, cmax_ref, bias_ref, m_ref, acc_ref, *, lambda_init):
    i = pl.program_id(2)
    blk = q_ref.shape[0]
    e = v_ref.shape[1]
    n_blocks = kb_ref.shape[0]
    diag_slot = 2

    def scores(kb, slot, diagonal=False):
        s = _scores(kb_ref[kb], q2_ref[...])
        if diagonal:
            s = s + bias_ref[...]
        s_ref[slot] = s
        cmax_ref[slot] = jnp.max(s, axis=0, keepdims=True)

    def accumulate(kb, slot):
        m_prev = m_ref[...]
        m_new = jnp.maximum(m_prev, cmax_ref[slot])
        alpha = jnp.exp(m_prev - m_new)
        p = jnp.exp(s_ref[slot] - m_new).astype(BF16)
        acc_ref[...] = alpha * acc_ref[...] + jnp.dot(vt_ref[kb], p, preferred_element_type=F32)
        m_ref[...] = m_new

    m_ref[...] = jnp.full_like(m_ref, MASK_VALUE)
    acc_ref[...] = jnp.zeros_like(acc_ref)

    def finish():
        q2_ref[...] = _stack_components(qn_ref[...])
        scores(0, 0)
        accumulate(i, diag_slot)
        lam = _lambda(lq1_ref, lk1_ref, lq2_ref, lk2_ref, lambda_init)
        o_ref[...] = _diff_finish(acc_ref[0:e, :], acc_ref[e:e + 1, :], lam, g_ref[...],
                                  lambda_init).astype(o_ref.dtype)

    @pl.when((pl.program_id(0) == 0) & (pl.program_id(1) == 0) & (i == 0))
    def _():
        bias_ref[...] = jnp.where(_visible(bias_ref.shape, 0, 0), 0.0, MASK_VALUE)

    @pl.when(i == 0)
    def _():
        ones = jnp.ones((ONES_ROWS, blk), BF16)
        for j in range(n_blocks):
            rows = slice(j * blk, (j + 1) * blk)
            kb_ref[j] = k_ref[rows, :].astype(BF16)
            vt_ref[j] = jnp.concatenate([v_ref[rows, :].T.astype(BF16), ones], axis=0)
        q2_ref[...] = _stack_components(q_ref[...])
        scores(0, diag_slot, diagonal=True)
        finish()

    unroll = ATTN_LOOP_UNROLL

    def trip(j, carry):
        for u in range(unroll):
            scores(unroll * j + u + 1, (u + 1) % 2)
            accumulate(unroll * j + u, u % 2)
        return carry

    lax.fori_loop(0, jnp.maximum(i - 1, 0) // unroll, trip, 0)

    for rest in range(1, unroll + 1):
        @pl.when((i >= 1) & ((i - 1) % unroll == rest - 1))
        def _(rest=rest):
            first = i - rest
            for u in range(rest):
                if u + 1 < rest:
                    scores(first + u + 1, (u + 1) % 2)
                else:
                    scores(i, diag_slot, diagonal=True)
                accumulate(first + u, u % 2)
            finish()


def _attn_prompt(q, k, v, lams, gain_col, n_batch, seq_len, n_heads, lambda_init):
    t, width = q.shape
    e = width // n_heads
    blk = ATTN_BLOCK
    assert seq_len % blk == 0 and blk % CHUNK == 0 and e == V7X_LANES
    nq = seq_len // blk
    lam_spec = pl.BlockSpec(lams[0].shape, lambda b, h, i: (0, 0))
    n_blocks = seq_len // blk
    scratch = [((n_blocks, blk, e), BF16), ((n_blocks, e + ONES_ROWS, blk), BF16), ((2 * blk, e), BF16),
               ((3, blk, 2 * blk), F32), ((3, 1, 2 * blk), F32), ((blk, 2 * blk), F32), ((1, 2 * blk), F32),
               ((e + ONES_ROWS, 2 * blk), F32)]
    vmem = _vmem_limit(
        pipelined=[((blk, e), BF16)] * 3 + [((seq_len, e), F32)] * 2,
        resident=scratch + [((blk, 2 * blk), F32)] * 6)
    return pl.pallas_call(
        functools.partial(_attn_prompt_kernel, lambda_init=lambda_init),
        name="attn_prompt",
        grid=(n_batch, n_heads, nq),
        in_specs=[
            pl.BlockSpec((blk, e), lambda b, h, i: (b * nq + i, h)),
            pl.BlockSpec((blk, e), lambda b, h, i: (b * nq + jnp.minimum(i + 1, nq - 1), h)),
            pl.BlockSpec((seq_len, e), lambda b, h, i: (b, h)),
            pl.BlockSpec((seq_len, e), lambda b, h, i: (b, h)),
            lam_spec, lam_spec, lam_spec, lam_spec,
            pl.BlockSpec(gain_col.shape, lambda b, h, i: (0, 0)),
        ],
        out_specs=pl.BlockSpec((blk, e), lambda b, h, i: (b * nq + i, h)),
        out_shape=jax.ShapeDtypeStruct((t, width), BF16),
        scratch_shapes=[pltpu.VMEM(s, d) for s, d in scratch],
        compiler_params=pltpu.CompilerParams(
            dimension_semantics=("arbitrary", "arbitrary", "arbitrary"), vmem_limit_bytes=vmem),
    )(q, q, k, v, *lams, gain_col)


def _attn_sample_kernel(q_ref, kn_ref, vn_ref, kc_ref, vc_ref, lq1_ref, lk1_ref, lq2_ref, lk2_ref,
                        g_ref, o_ref, *, n_heads, past_len, lambda_init):
    rows = q_ref.shape[0]
    e = q_ref.shape[1] // n_heads
    lam = _lambda(lq1_ref, lk1_ref, lq2_ref, lk2_ref, lambda_init)
    gain_col = g_ref[...]
    visible = _visible((rows, 2 * rows), past_len, past_len)
    for hd in range(n_heads):
        cols = slice(hd * e, (hd + 1) * e)
        kc = kc_ref[pl.ds(hd, past_len, stride=n_heads), :].astype(BF16)
        vc = vc_ref[pl.ds(hd, past_len, stride=n_heads), :].astype(BF16)
        q2 = _stack_components(q_ref[:, cols])
        s_c = _scores(kc, q2)
        s_n = jnp.where(visible, _scores(kn_ref[:, cols].astype(BF16), q2), MASK_VALUE)
        m = jnp.maximum(jnp.max(s_c, axis=0, keepdims=True), jnp.max(s_n, axis=0, keepdims=True))
        p_c = jnp.exp(s_c - m)
        p_n = jnp.exp(s_n - m)
        l = jnp.sum(p_c, axis=0, keepdims=True) + jnp.sum(p_n, axis=0, keepdims=True)
        acc = (_weighted_values(vc, p_c.astype(BF16))
               + _weighted_values(vn_ref[:, cols].astype(BF16), p_n.astype(BF16)))
        o_ref[:, cols] = _diff_finish(acc, l, lam, gain_col, lambda_init).astype(o_ref.dtype)


def _attn_sample(q, k_new, v_new, k_cache, v_cache, lams, gain_col, n_batch, seq_len, n_heads,
                 lambda_init):
    t, width = q.shape
    e = width // n_heads
    past_len = k_cache.shape[0] // (n_batch * n_heads)
    assert seq_len % 8 == 0 and past_len % 8 == 0 and k_cache.shape[1] == e
    tok = lambda b: (b, 0)
    lam_spec = pl.BlockSpec(lams[0].shape, lambda b: (0, 0))
    vmem = _vmem_limit(
        pipelined=[((past_len * n_heads, e), F32)] * 2 + [((seq_len, width), F32)] * 4,
        resident=[((past_len, 2 * seq_len), F32)] * 4 + [((past_len, e), F32)] * 4)
    return pl.pallas_call(
        functools.partial(_attn_sample_kernel, n_heads=n_heads, past_len=past_len,
                          lambda_init=lambda_init),
        name="attn_sample",
        grid=(n_batch,),
        in_specs=[
            pl.BlockSpec((seq_len, width), tok),
            pl.BlockSpec((seq_len, width), tok),
            pl.BlockSpec((seq_len, width), tok),
            pl.BlockSpec((past_len * n_heads, e), tok),
            pl.BlockSpec((past_len * n_heads, e), tok),
            lam_spec, lam_spec, lam_spec, lam_spec,
            pl.BlockSpec(gain_col.shape, lambda b: (0, 0)),
        ],
        out_specs=pl.BlockSpec((seq_len, width), tok),
        out_shape=jax.ShapeDtypeStruct((t, width), BF16),
        compiler_params=pltpu.CompilerParams(
            dimension_semantics=("parallel",), vmem_limit_bytes=vmem),
    )(q, k_new, v_new, k_cache, v_cache, *lams, gain_col)


def _rope_tables(pos, head_dim, rows):
    half = head_dim // 2
    inv = ROPE_THETA ** (-jnp.arange(0, head_dim, 2, dtype=F32) / head_dim)
    ang = pos.astype(F32)[:, None] * inv[None, :]
    cos, sin = jnp.cos(ang), jnp.sin(ang)
    zero = jnp.zeros_like(sin)
    reps = V7X_LANES // head_dim
    cos_t = jnp.tile(jnp.concatenate([cos, cos], axis=1), (rows // pos.shape[0], reps))
    sin_lo = jnp.tile(jnp.concatenate([-sin, zero], axis=1), (rows // pos.shape[0], reps))
    sin_hi = jnp.tile(jnp.concatenate([zero, sin], axis=1), (rows // pos.shape[0], reps))
    return cos_t, sin_lo, sin_hi


def kernel(x_prompt, x_sample, cache_k, cache_v, state_conv, ffn1_norm, ffn1_w_gate, ffn1_w_up,
           ffn1_w_down, mix_norm, w_in, lambda_q1, lambda_k1, lambda_q2, lambda_k2, subln_gain,
           conv_w, w_out, ffn2_norm, ffn2_w_gate, ffn2_w_up, ffn2_w_down, final_norm):
    n_bp, seq_p, d = x_prompt.shape
    n_bs, seq_s, _ = x_sample.shape
    depth, _, past_len, n_heads, key_dim = cache_k.shape
    value_dim = cache_v.shape[-1]
    head_dim = key_dim // 2
    conv_ch = conv_w.shape[-1]
    attn_width = n_heads * key_dim
    assert key_dim == V7X_LANES and value_dim == V7X_LANES and attn_width == conv_ch
    q_scale = head_dim ** -0.5

    tabs_p = _rope_tables(jnp.arange(seq_p), head_dim, seq_p)
    tabs_s = _rope_tables(past_len + jnp.arange(seq_s), head_dim, max(seq_s, INPROJ_TOKEN_TILE))

    xp = x_prompt.reshape(n_bp * seq_p, d)
    xs = x_sample.reshape(n_bs * seq_s, d)
    row = lambda a: a.reshape(1, -1)
    outs = {name: [] for name in ("kp", "vp", "cp", "ks", "vs", "cs")}
    for l in range(depth):
        lambda_init = 0.8 - 0.6 * math.exp(-0.3 * l)
        last = l == depth - 1
        w1g, w1u, w1d = (_to_bf16(w[l]) for w in (ffn1_w_gate, ffn1_w_up, ffn1_w_down))
        w2g, w2u, w2d = (_to_bf16(w[l]) for w in (ffn2_w_gate, ffn2_w_up, ffn2_w_down))
        w_in_l, w_out_l = _to_bf16(w_in[l]), _to_bf16(w_out[l])
        lams = tuple(row(a[l]) for a in (lambda_q1, lambda_k1, lambda_q2, lambda_k2))
        gain = subln_gain[l].reshape(-1, 1)
        fin = row(final_norm)
        zeros_state = jnp.zeros((n_bp, conv_w.shape[1] - 1, conv_ch), F32)

        def layer(x, tabs, state, seq_len, attend):
            x = _ffn(x, row(ffn1_norm[l]), w1g, w1u, w1d, fin, False)
            q, k, v, conv_out, new_state = _inproj(
                x, row(mix_norm[l]), w_in_l, tabs, conv_w[l], state, seq_len, n_heads, q_scale)
            attn_out = attend(q, k, v)
            x = _outproj(x, attn_out, conv_out, w_out_l)
            x = _ffn(x, row(ffn2_norm[l]), w2g, w2u, w2d, fin, last)
            return x, k, v, new_state

        xp, kp, vp, cp = layer(
            xp, tabs_p, zeros_state, seq_p,
            lambda q, k, v: _attn_prompt(q, k, v, lams, gain, n_bp, seq_p, n_heads, lambda_init))
        kc = cache_k[l].reshape(n_bs * past_len * n_heads, key_dim)
        vc = cache_v[l].reshape(n_bs * past_len * n_heads, value_dim)
        xs, ks, vs, cs = layer(
            xs, tabs_s, state_conv[l], seq_s,
            lambda q, k, v: _attn_sample(q, k, v, kc, vc, lams, gain, n_bs, seq_s, n_heads,
                                         lambda_init))
        outs["kp"].append(kp.reshape(n_bp, seq_p, n_heads, key_dim))
        outs["vp"].append(vp.reshape(n_bp, seq_p, n_heads, value_dim))
        outs["cp"].append(cp)
        outs["ks"].append(ks.reshape(n_bs, seq_s, n_heads, key_dim))
        outs["vs"].append(vs.reshape(n_bs, seq_s, n_heads, value_dim))
        outs["cs"].append(cs)
    return (xp.reshape(x_prompt.shape), xs.reshape(x_sample.shape),
            jnp.stack(outs["kp"]), jnp.stack(outs["vp"]), jnp.stack(outs["cp"]),
            jnp.stack(outs["ks"]), jnp.stack(outs["vs"]), jnp.stack(outs["cs"]))
```
